```python
import math
import jax
import jax.numpy as jnp
from jax import lax
import numpy as np

D_MODEL = 2048
BATCH = 2
SEQ = 16384
DEPTH = 1

SSM_HEAD_DIM = 64
SSM_WIDTH = D_MODEL
SSM_HEADS = SSM_WIDTH // SSM_HEAD_DIM
SSM_GROUPS = 4
SSM_HEADS_PER_GROUP = SSM_HEADS // SSM_GROUPS
SSM_STATE = 128
CONV_WIDTH = 5
CONV_DIM = SSM_WIDTH + 2 * SSM_GROUPS * SSM_STATE
SSD_CHUNK = 128
ATT_HEADS = 8
ATT_HEAD_DIM = D_MODEL // ATT_HEADS // 2
ATT_V_DIM = 2 * ATT_HEAD_DIM
ATT_QK_WIDTH = ATT_HEADS * 2 * ATT_HEAD_DIM
ATT_WIDTH = ATT_HEADS * ATT_V_DIM
ROPE_THETA = 500000.0
ROT_DIM = ATT_HEAD_DIM // 4
Q_BLOCK = 128
D_FF = ((8 * D_MODEL // 3 + 255) // 256) * 256
DEEPNORM_ALPHA = (2 * DEPTH) ** 0.25
DEEPNORM_BETA = (8 * DEPTH) ** -0.25
NORM_EPS = 1e-5
IN_SIZES = (SSM_WIDTH, CONV_DIM, SSM_HEADS, SSM_HEADS, ATT_QK_WIDTH, ATT_QK_WIDTH, ATT_WIDTH, D_MODEL, D_MODEL)
IN_OFFSETS = tuple(sum(IN_SIZES[:i]) for i in range(len(IN_SIZES) + 1))
D_IN = IN_OFFSETS[-1]

kernel_name = 'hybrid_ssd_diffattn_macaron_deepnorm_encoder'


def _layer_norm(x, g, b):
    xf = x.astype(jnp.float32)
    mu = jnp.mean(xf, axis=-1, keepdims=True)
    var = jnp.mean(jnp.square(xf - mu), axis=-1, keepdims=True)
    y = (xf - mu) * lax.rsqrt(var + NORM_EPS)
    return (y * g.astype(jnp.float32) + b.astype(jnp.float32)).astype(x.dtype)


def _swiglu(x, w_gate, w_up, w_down):
    return (jax.nn.silu(x @ w_gate) * (x @ w_up)) @ w_down


def _centred_depthwise_conv(u, w, b):
    pad = CONV_WIDTH // 2
    y = lax.conv_general_dilated(u, w, window_strides=(1,), padding=[(pad, pad)],
                                 dimension_numbers=('NWC', 'WIO', 'NWC'),
                                 feature_group_count=u.shape[-1])
    return y + b


def _ssd_scan(xh, dt, a_coef, bm, cm):
    bsz, seq, g, j, p = xh.shape
    n = bm.shape[-1]
    l = SSD_CHUNK
    c = seq // l
    xdt = (xh * dt[..., None]).reshape(bsz, c, l, g, j, p)
    a = (dt * a_coef).reshape(bsz, c, l, g, j).transpose(0, 1, 3, 4, 2)
    a_cum = jnp.cumsum(a, axis=-1)
    bc = bm.reshape(bsz, c, l, g, n)
    cc = cm.reshape(bsz, c, l, g, n)
    seg = a_cum[..., :, None] - a_cum[..., None, :]
    lower = jnp.tril(jnp.ones((l, l), dtype=bool))
    decay = jnp.exp(jnp.where(lower, seg, -jnp.inf))
    cb = jnp.einsum('bclgn,bcsgn->bcgls', cc, bc)
    y_diag = jnp.einsum('bcgjls,bcsgjp->bclgjp', cb[:, :, :, None] * decay, xdt)
    decay_to_end = jnp.exp(a_cum[..., -1:] - a_cum)
    states = jnp.einsum('bclgn,bcgjl,bclgjp->bcgjpn', bc, decay_to_end, xdt)
    chunk_decay = jnp.exp(a_cum[..., -1])

    def step(h, inp):
        st, dec = inp
        return dec[..., None, None] * h + st, h

    h0 = jnp.zeros((bsz, g, j, p, n), dtype=xdt.dtype)
    _, prev = lax.scan(step, h0, (jnp.moveaxis(states, 1, 0), jnp.moveaxis(chunk_decay, 1, 0)))
    prev = jnp.moveaxis(prev, 0, 1)
    y_off = jnp.einsum('bclgn,bcgjpn,bcgjl->bclgjp', cc, prev, jnp.exp(a_cum))
    return (y_diag + y_off).reshape(bsz, seq, g, j, p)


def _ssd_branch(z, xbc, dt_f_raw, dt_b_raw, conv_w, conv_b, dt_bias_f, dt_bias_b,
                a_log_f, a_log_b, d_skip, norm_w, w_br):
    f32 = jnp.float32
    bsz, seq, _ = z.shape
    xbc = jax.nn.silu(_centred_depthwise_conv(xbc, conv_w, conv_b))
    gn = SSM_GROUPS * SSM_STATE
    xh = xbc[..., :SSM_WIDTH].reshape(bsz, seq, SSM_GROUPS, SSM_HEADS_PER_GROUP, SSM_HEAD_DIM).astype(f32)
    bm = xbc[..., SSM_WIDTH:SSM_WIDTH + gn].reshape(bsz, seq, SSM_GROUPS, SSM_STATE).astype(f32)
    cm = xbc[..., SSM_WIDTH + gn:].reshape(bsz, seq, SSM_GROUPS, SSM_STATE).astype(f32)

    def direction(dt_raw, dt_bias, a_log, reverse):
        dt = jax.nn.softplus(dt_raw.astype(f32) + dt_bias.astype(f32))
        dt = dt.reshape(bsz, seq, SSM_GROUPS, SSM_HEADS_PER_GROUP)
        a_coef = -jnp.exp(a_log.astype(f32)).reshape(SSM_GROUPS, SSM_HEADS_PER_GROUP)
        if reverse:
            flip = lambda t: jnp.flip(t, axis=1)
            return flip(_ssd_scan(flip(xh), flip(dt), a_coef, flip(bm), flip(cm)))
        return _ssd_scan(xh, dt, a_coef, bm, cm)

    d = d_skip.astype(f32).reshape(SSM_GROUPS, SSM_HEADS_PER_GROUP)[..., None]
    y = direction(dt_f_raw, dt_bias_f, a_log_f, False) + direction(dt_b_raw, dt_bias_b, a_log_b, True) + d * xh
    gsz = SSM_WIDTH // SSM_GROUPS
    y = y.reshape(bsz, seq, SSM_GROUPS, gsz) * jax.nn.silu(z.astype(f32)).reshape(bsz, seq, SSM_GROUPS, gsz)
    y = y * lax.rsqrt(jnp.mean(y * y, axis=-1, keepdims=True) + NORM_EPS)
    y = y.reshape(bsz, seq, SSM_WIDTH) * norm_w.astype(f32)
    return y.astype(z.dtype) @ w_br


def _partial_rope(t, cos, sin):
    half = ROT_DIM // 2
    t1 = t[..., :half].astype(jnp.float32)
    t2 = t[..., half:ROT_DIM].astype(jnp.float32)
    rot = jnp.concatenate([t1 * cos - t2 * sin, t2 * cos + t1 * sin], axis=-1).astype(t.dtype)
    return jnp.concatenate([rot, t[..., ROT_DIM:]], axis=-1)


def _diff_attention_branch(q, k, v, lq1, lk1, lq2, lk2, norm_w, w_br, lambda_init):
    f32 = jnp.float32
    bsz, seq, _ = q.shape
    q = q.reshape(bsz, seq, ATT_HEADS, 2, ATT_HEAD_DIM)
    k = k.reshape(bsz, seq, ATT_HEADS, 2, ATT_HEAD_DIM)
    v = v.reshape(bsz, seq, ATT_HEADS, ATT_V_DIM)
    pos = jnp.arange(seq, dtype=f32)
    inv_freq = ROPE_THETA ** (-jnp.arange(0, ROT_DIM, 2, dtype=f32) / ROT_DIM)
    ang = pos[:, None] * inv_freq[None, :]
    cos = jnp.cos(ang)[:, None, None, :]
    sin = jnp.sin(ang)[:, None, None, :]
    q = _partial_rope(q, cos, sin).transpose(0, 2, 3, 1, 4)
    k = _partial_rope(k, cos, sin).transpose(0, 2, 3, 1, 4)
    v = v.transpose(0, 2, 1, 3)
    lam = (jnp.exp(jnp.sum(lq1.astype(f32) * lk1.astype(f32)))
           - jnp.exp(jnp.sum(lq2.astype(f32) * lk2.astype(f32))) + lambda_init)
    scale = ATT_HEAD_DIM ** -0.5
    n_blocks = seq // Q_BLOCK
    q_blocks = jnp.moveaxis(q.reshape(bsz, ATT_HEADS, 2, n_blocks, Q_BLOCK, ATT_HEAD_DIM), 3, 0)

    def attend(qb):
        s = jnp.einsum('bhmqd,bhmkd->bhmqk', qb, k).astype(f32) * scale
        p = jax.nn.softmax(s, axis=-1)
        w = p[:, :, 0] - lam * p[:, :, 1]
        return jnp.einsum('bhqk,bhkv->bhqv', w.astype(v.dtype), v)

    o = lax.map(attend, q_blocks)
    o = o.transpose(1, 0, 3, 2, 4).reshape(bsz, seq, ATT_HEADS, ATT_V_DIM).astype(f32)
    o = o * lax.rsqrt(jnp.mean(o * o, axis=-1, keepdims=True) + NORM_EPS)
    o = o * norm_w.astype(f32) * (1.0 - lambda_init)
    return o.reshape(bsz, seq, ATT_WIDTH).astype(q.dtype) @ w_br


def _hybrid_mixer(h, w_in, conv_w, conv_b, dt_bias_f, dt_bias_b, a_log_f, a_log_b, d_skip,
                  ssm_norm_w, w_ssm_br, lambda_q1, lambda_k1, lambda_q2, lambda_k2,
                  attn_norm_w, w_att_br, w_out, lambda_init):
    z, xbc, dt_f, dt_b, q, k, v, g_ssm, g_att = [
        h @ w_in[:, lo:hi] for lo, hi in zip(IN_OFFSETS[:-1], IN_OFFSETS[1:])]
    y_ssm = _ssd_branch(z, xbc, dt_f, dt_b, conv_w, conv_b, dt_bias_f, dt_bias_b,
                        a_log_f, a_log_b, d_skip, ssm_norm_w, w_ssm_br)
    y_att = _diff_attention_branch(q, k, v, lambda_q1, lambda_k1, lambda_q2, lambda_k2,
                                   attn_norm_w, w_att_br, lambda_init)
    merged = jax.nn.sigmoid(g_ssm) * y_ssm + jax.nn.sigmoid(g_att) * y_att
    return merged @ w_out


def _dt_bias(k):
    dt = jnp.exp(jax.random.uniform(k, (DEPTH, SSM_HEADS), jnp.float32,
                                    minval=math.log(1e-3), maxval=math.log(1e-1)))
    return dt + jnp.log(-jnp.expm1(-dt))


def _a_log(k):
    return jnp.log(jax.random.uniform(k, (DEPTH, SSM_HEADS), jnp.float32, minval=1.0, maxval=16.0))


def setup_inputs(seed: int = 0) -> dict:
    key = jax.random.key(seed)
    ks = iter(jax.random.split(key, 32))

    def nrm(shape, scale):
        return jax.random.normal(next(ks), (DEPTH,) + shape, jnp.float32) * scale

    def gain(shape):
        return 1.0 + nrm(shape, 0.02)

    x = jax.random.normal(next(ks), (BATCH, SEQ, D_MODEL), jnp.float32)
    return {
        'x': x,
        'ffn1_w_gate': nrm((D_MODEL, D_FF), D_MODEL ** -0.5),
        'ffn1_w_up': nrm((D_MODEL, D_FF), D_MODEL ** -0.5),
        'ffn1_w_down': nrm((D_FF, D_MODEL), D_FF ** -0.5 * DEEPNORM_BETA),
        'ln1_g': gain((D_MODEL,)),
        'ln1_b': nrm((D_MODEL,), 0.02),
        'w_in': nrm((D_MODEL, D_IN), D_MODEL ** -0.5),
        'conv_w': nrm((CONV_WIDTH, 1, CONV_DIM), CONV_WIDTH ** -0.5),
        'conv_b': nrm((CONV_DIM,), 0.02),
        'dt_bias_f': _dt_bias(next(ks)),
        'dt_bias_b': _dt_bias(next(ks)),
        'a_log_f': _a_log(next(ks)),
        'a_log_b': _a_log(next(ks)),
        'd_skip': gain((SSM_HEADS,)),
        'ssm_norm_w': gain((SSM_WIDTH,)),
        'w_ssm_br': nrm((SSM_WIDTH, D_MODEL), SSM_WIDTH ** -0.5),
        'lambda_q1': nrm((ATT_HEAD_DIM,), 0.1),
        'lambda_k1': nrm((ATT_HEAD_DIM,), 0.1),
        'lambda_q2': nrm((ATT_HEAD_DIM,), 0.1),
        'lambda_k2': nrm((ATT_HEAD_DIM,), 0.1),
        'attn_norm_w': gain((ATT_V_DIM,)),
        'w_att_br': nrm((ATT_WIDTH, D_MODEL), ATT_WIDTH ** -0.5),
        'w_out': nrm((D_MODEL, D_MODEL), D_MODEL ** -0.5 * DEEPNORM_BETA),
        'ln2_g': gain((D_MODEL,)),
        'ln2_b': nrm((D_MODEL,), 0.02),
        'ffn2_w_gate': nrm((D_MODEL, D_FF), D_MODEL ** -0.5),
        'ffn2_w_up': nrm((D_MODEL, D_FF), D_MODEL ** -0.5),
        'ffn2_w_down': nrm((D_FF, D_MODEL), D_FF ** -0.5 * DEEPNORM_BETA),
        'ln3_g': gain((D_MODEL,)),
        'ln3_b': nrm((D_MODEL,), 0.02),
    }


def reference(x, ffn1_w_gate, ffn1_w_up, ffn1_w_down, ln1_g, ln1_b, w_in, conv_w, conv_b,
              dt_bias_f, dt_bias_b, a_log_f, a_log_b, d_skip, ssm_norm_w, w_ssm_br,
              lambda_q1, lambda_k1, lambda_q2, lambda_k2, attn_norm_w, w_att_br, w_out,
              ln2_g, ln2_b, ffn2_w_gate, ffn2_w_up, ffn2_w_down, ln3_g, ln3_b):
    for l in range(DEPTH):
        lambda_init = 0.8 - 0.6 * math.exp(-0.3 * l)
        x = _layer_norm(DEEPNORM_ALPHA * x + 0.5 * _swiglu(x, ffn1_w_gate[l], ffn1_w_up[l], ffn1_w_down[l]),
                        ln1_g[l], ln1_b[l])
        mix = _hybrid_mixer(x, w_in[l], conv_w[l], conv_b[l], dt_bias_f[l], dt_bias_b[l], a_log_f[l], a_log_b[l],
                            d_skip[l], ssm_norm_w[l], w_ssm_br[l], lambda_q1[l], lambda_k1[l], lambda_q2[l],
                            lambda_k2[l], attn_norm_w[l], w_att_br[l], w_out[l], lambda_init)
        x = _layer_norm(DEEPNORM_ALPHA * x + mix, ln2_g[l], ln2_b[l])
        x = _layer_norm(DEEPNORM_ALPHA * x + 0.5 * _swiglu(x, ffn2_w_gate[l], ffn2_w_up[l], ffn2_w_down[l]),
                        ln3_g[l], ln3_b[l])
    return x
```

```python
import functools
import math

import jax
import jax.numpy as jnp
from jax import lax
from jax.experimental import pallas as pl
from jax.experimental.pallas import tpu as pltpu

F32 = jnp.float32
BF16 = jnp.bfloat16

D_MODEL = 2048
SSM_HEAD_DIM = 64
SSM_HEADS = 32
SSM_GROUPS = 4
SSM_STATE = 128
CONV_WIDTH = 5
CONV_DIM = 3072
SSD_CHUNK = 128
ATT_HEADS = 8
ATT_HEAD_DIM = 128
ATT_V_DIM = 256
ROPE_THETA = 500000.0
ROT_DIM = 32
D_FF = 5632
DEPTH = 1
ALPHA = (2 * DEPTH) ** 0.25
NORM_EPS = 1e-5
LAMBDA_INIT = 0.8 - 0.6 * math.exp(-0.3 * 0)

LANES = 128
VMEM_LIMIT = 56 * 2 ** 20

COL_Q, COL_K, COL_V, COL_Z, COL_GS, COL_GA, COL_XBC = 0, 2048, 4096, 6144, 8192, 10240, 12288
N_MAIN = 15360
Q_SCALE = ATT_HEAD_DIM ** -0.5 * math.log2(math.e)


def _params(*sem):
    return pltpu.CompilerParams(dimension_semantics=sem, vmem_limit_bytes=VMEM_LIMIT)


def _layer_norm(y, g, b):
    mu = jnp.mean(y, axis=-1, keepdims=True)
    yc = y - mu
    var = jnp.mean(yc * yc, axis=-1, keepdims=True)
    return yc * lax.rsqrt(var + NORM_EPS) * g + b


def _sigmoid(v):
    return 1.0 / (1.0 + jnp.exp(-v))


def _ffn_ln_kernel(x_ref, wg_ref, wu_ref, wd_ref, g_ref, b_ref, *rest, n_f, emit_bf16):
    if emit_bf16:
        o_ref, ob_ref, xb_ref, acc_ref = rest
    else:
        o_ref, xb_ref, acc_ref = rest
    f = pl.program_id(1)

    @pl.when(f == 0)
    def _init():
        xb_ref[...] = x_ref[...].astype(BF16)
        acc_ref[...] = jnp.zeros_like(acc_ref)

    xb = xb_ref[...]
    hg = jnp.dot(xb, wg_ref[...], preferred_element_type=F32)
    hu = jnp.dot(xb, wu_ref[...], preferred_element_type=F32)
    h = hg * _sigmoid(hg) * hu
    acc_ref[...] += jnp.dot(h.astype(BF16), wd_ref[...], preferred_element_type=F32)

    @pl.when(f == n_f - 1)
    def _fin():
        y = ALPHA * x_ref[...] + 0.5 * acc_ref[...]
        out = _layer_norm(y, g_ref[...], b_ref[...])
        o_ref[...] = out
        if emit_bf16:
            ob_ref[...] = out.astype(BF16)


def _ffn_ln(x, wg, wu, wd, g, b, *, emit_bf16, tm=512, tf=512):
    t, d = x.shape
    dff = wg.shape[1]
    tm = min(tm, t)
    n_f = dff // tf
    out_shape = [jax.ShapeDtypeStruct((t, d), F32)]
    out_specs = [pl.BlockSpec((tm, d), lambda i, f: (i, 0))]
    if emit_bf16:
        out_shape.append(jax.ShapeDtypeStruct((t, d), BF16))
        out_specs.append(pl.BlockSpec((tm, d), lambda i, f: (i, 0)))
    return pl.pallas_call(
        functools.partial(_ffn_ln_kernel, n_f=n_f, emit_bf16=emit_bf16),
        grid=(t // tm, n_f),
        in_specs=[
            pl.BlockSpec((tm, d), lambda i, f: (i, 0)),
            pl.BlockSpec((d, tf), lambda i, f: (0, f)),
            pl.BlockSpec((d, tf), lambda i, f: (0, f)),
            pl.BlockSpec((tf, d), lambda i, f: (f, 0)),
            pl.BlockSpec((1, d), lambda i, f: (0, 0)),
            pl.BlockSpec((1, d), lambda i, f: (0, 0)),
        ],
        out_specs=out_specs,
        out_shape=out_shape,
        scratch_shapes=[pltpu.VMEM((tm, d), BF16), pltpu.VMEM((tm, d), F32)],
        compiler_params=_params("parallel", "arbitrary"),
        name="ffn_ln",
    )(x, wg, wu, wd, g, b)


def _inproj_kernel(x_ref, w_ref, cos_ref, s1_ref, s2_ref, o_ref, *, n_q, n_qk, tn):
    j = pl.program_id(1)
    y = jnp.dot(x_ref[...], w_ref[...], preferred_element_type=F32)

    @pl.when(j >= n_qk)
    def _plain():
        o_ref[...] = y.astype(o_ref.dtype)

    @pl.when(j < n_qk)
    def _rope():
        sc = jnp.where(j < n_q, Q_SCALE, 1.0).astype(F32)
        c = cos_ref[...] * sc
        s1 = s1_ref[...] * sc
        s2 = s2_ref[...] * sc
        half = ROT_DIM // 2
        for k in range(tn // LANES):
            yk = y[:, k * LANES:(k + 1) * LANES]
            r = yk * c + pltpu.roll(yk, half, 1) * s1 + pltpu.roll(yk, LANES - half, 1) * s2
            o_ref[:, k * LANES:(k + 1) * LANES] = r.astype(o_ref.dtype)


def _in_proj(hb, w_main, cos_t, s1_t, s2_t, *, seq, tm=1024, tn=512):
    t, d = hb.shape
    n = w_main.shape[1]
    tm = min(tm, seq)
    tiles_per_seq = seq // tm
    tab = pl.BlockSpec((tm, LANES), lambda i, j: (i % tiles_per_seq, 0))
    return pl.pallas_call(
        functools.partial(_inproj_kernel, n_q=COL_K // tn, n_qk=COL_V // tn, tn=tn),
        grid=(t // tm, n // tn),
        in_specs=[
            pl.BlockSpec((tm, d), lambda i, j: (i, 0)),
            pl.BlockSpec((d, tn), lambda i, j: (0, j)),
            tab, tab, tab,
        ],
        out_specs=pl.BlockSpec((tm, tn), lambda i, j: (i, j)),
        out_shape=jax.ShapeDtypeStruct((t, n), BF16),
        compiler_params=_params("parallel", "arbitrary"),
        name="in_proj",
    )(hb, w_main, cos_t, s1_t, s2_t)


def _dt_proj_kernel(x_ref, w_ref, o_ref):
    o_ref[...] = jnp.dot(x_ref[...], w_ref[...], preferred_element_type=F32)


def _dt_proj(hb, w_dt, *, tm=1024):
    t, d = hb.shape
    n = w_dt.shape[1]
    tm = min(tm, t)
    return pl.pallas_call(
        _dt_proj_kernel,
        grid=(t // tm,),
        in_specs=[pl.BlockSpec((tm, d), lambda i: (i, 0)), pl.BlockSpec((d, n), lambda i: (0, 0))],
        out_specs=pl.BlockSpec((tm, n), lambda i: (i, 0)),
        out_shape=jax.ShapeDtypeStruct((t, n), F32),
        compiler_params=_params("parallel"),
        name="dt_proj",
    )(hb, w_dt)


HALO = 16


def _conv_kernel(cur_ref, prev_ref, next_ref, w_ref, b_ref, o_ref, ext_ref, *, tm, tiles_per_seq):
    i = pl.program_id(0)
    first = (i % tiles_per_seq) == 0
    last = (i % tiles_per_seq) == tiles_per_seq - 1
    ext_ref[0:HALO, :] = jnp.where(first, 0.0, prev_ref[...].astype(F32))
    ext_ref[HALO:HALO + tm, :] = cur_ref[...].astype(F32)
    ext_ref[HALO + tm:2 * HALO + tm, :] = jnp.where(last, 0.0, next_ref[...].astype(F32))
    pad = CONV_WIDTH // 2
    acc = b_ref[...] + ext_ref[HALO - pad:HALO - pad + tm, :] * w_ref[0:1, :]
    for w in range(1, CONV_WIDTH):
        acc = acc + ext_ref[HALO - pad + w:HALO - pad + w + tm, :] * w_ref[w:w + 1, :]
    o_ref[...] = (acc * _sigmoid(acc)).astype(o_ref.dtype)


def _conv_silu(main, conv_w, conv_b, *, seq, tm=256):
    t = main.shape[0]
    tm = min(tm, seq)
    tiles_per_seq = seq // tm
    cb = COL_XBC // CONV_DIM
    r = tm // HALO
    nblk = t // HALO
    return pl.pallas_call(
        functools.partial(_conv_kernel, tm=tm, tiles_per_seq=tiles_per_seq),
        grid=(t // tm,),
        in_specs=[
            pl.BlockSpec((tm, CONV_DIM), lambda i: (i, cb)),
            pl.BlockSpec((HALO, CONV_DIM), lambda i: (jnp.maximum(i * r - 1, 0), cb)),
            pl.BlockSpec((HALO, CONV_DIM), lambda i: (jnp.minimum((i + 1) * r, nblk - 1), cb)),
            pl.BlockSpec((8, CONV_DIM), lambda i: (0, 0)),
            pl.BlockSpec((1, CONV_DIM), lambda i: (0, 0)),
        ],
        out_specs=pl.BlockSpec((tm, CONV_DIM), lambda i: (i, 0)),
        out_shape=jax.ShapeDtypeStruct((t, CONV_DIM), BF16),
        scratch_shapes=[pltpu.VMEM((tm + 2 * HALO, CONV_DIM), F32)],
        compiler_params=_params("parallel"),
        name="conv_silu",
    )(main, main, main, conv_w, conv_b)


def _split_dot(v, m01, terms):
    out = None
    r = v
    for _ in range(terms):
        part = r.astype(BF16)
        d = jnp.dot(part, m01, preferred_element_type=F32)
        out = d if out is None else out + d
        r = r - part.astype(F32)
    return out


def _ssd_kernel(xbc_ref, dt_ref, par_ref, e_ref, y_ref, h_ref):
    L = SSD_CHUNK
    d = pl.program_id(1)
    c = pl.program_id(2)

    @pl.when(c == 0)
    def _init():
        h_ref[...] = jnp.zeros_like(h_ref)

    fwd = d == 0
    pre = dt_ref[...] + par_ref[0:1, :]
    dt = jnp.maximum(pre, 0.0) + jnp.log(1.0 + jnp.exp(-jnp.abs(pre)))
    a = dt * (-jnp.exp(par_ref[1:2, :]))

    row = lax.broadcasted_iota(jnp.int32, (L, L), 0)
    col = lax.broadcasted_iota(jnp.int32, (L, L), 1)
    msk = (col - row) * (1 - 2 * d) <= 0
    tri = jnp.where(msk, 1.0, 0.0).astype(BF16)
    c_col = _tri_cumsum(tri, a)
    c_row = c_col.T
    tot = jnp.where(fwd, c_col[L - 1:L, :], c_col[0:1, :])
    e_out = jnp.exp(c_col)
    f_in = dt * jnp.exp(tot - c_col)
    cd = jnp.broadcast_to(jnp.exp(tot), (8, LANES))
    stacked = jnp.concatenate([dt, f_in, e_out, cd], axis=0)
    ex = _split_dot(stacked, e_ref[...], 2)
    dt_x = ex[0:L]
    f_x = ex[L:2 * L]
    e_x = ex[2 * L:3 * L]
    cd_x = ex[3 * L:3 * L + 1]

    n_x = SSM_HEADS * SSM_HEAD_DIM
    gn = SSM_GROUPS * SSM_STATE
    xf = xbc_ref[:, 0:n_x].astype(F32)
    xdt = (xf * dt_x).astype(BF16)
    xw = (xf * f_x).astype(BF16)
    lane = lax.broadcasted_iota(jnp.int32, (L, LANES), 1)
    gw = n_x // SSM_GROUPS
    for g in range(SSM_GROUPS):
        bg = xbc_ref[:, n_x + g * SSM_STATE:n_x + (g + 1) * SSM_STATE]
        cg = xbc_ref[:, n_x + gn + g * SSM_STATE:n_x + gn + (g + 1) * SSM_STATE]
        bgt = bg.astype(F32).T.astype(BF16)
        gmat = jnp.dot(cg, bgt, preferred_element_type=F32)
        hg = h_ref[:, g * gw:(g + 1) * gw]
        yoff = jnp.dot(cg, hg.astype(BF16), preferred_element_type=F32) * e_x[:, g * gw:(g + 1) * gw]
        st = jnp.dot(bgt, xw[:, g * gw:(g + 1) * gw], preferred_element_type=F32)
        h_ref[:, g * gw:(g + 1) * gw] = cd_x[:, g * gw:(g + 1) * gw] * hg + st
        for jp in range(gw // LANES):
            lo = g * gw + jp * LANES
            xp = xdt[:, lo:lo + LANES]
            outs = []
            for hh in range(2):
                h = lo // SSM_HEAD_DIM + hh
                seg = c_col[:, h:h + 1] - c_row[h:h + 1, :]
                decay = jnp.exp(jnp.where(msk, seg, -jnp.inf))
                m = (gmat * decay).astype(BF16)
                outs.append(jnp.dot(m, xp, preferred_element_type=F32))
            ydiag = jnp.where(lane < SSM_HEAD_DIM, outs[0], outs[1])
            y_ref[:, lo:lo + LANES] = (ydiag + yoff[:, jp * LANES:(jp + 1) * LANES]).astype(y_ref.dtype)


def _tri_cumsum(tri, a):
    out = None
    r = a
    for _ in range(3):
        part = r.astype(BF16)
        d = jnp.dot(tri, part, preferred_element_type=F32)
        out = d if out is None else out + d
        r = r - part.astype(F32)
    return out


def _ssd(xbc_act, dt_raw, par, expand, *, batch, seq):
    t = xbc_act.shape[0]
    L = SSD_CHUNK
    nc = seq // L
    n_x = SSM_HEADS * SSM_HEAD_DIM

    def chunk(b, d, c):
        return b * nc + c + d * (nc - 1 - 2 * c)

    return pl.pallas_call(
        _ssd_kernel,
        grid=(batch, 2, nc),
        in_specs=[
            pl.BlockSpec((L, CONV_DIM), lambda b, d, c: (chunk(b, d, c), 0)),
            pl.BlockSpec((L, LANES), lambda b, d, c: (chunk(b, d, c), d)),
            pl.BlockSpec((8, LANES), lambda b, d, c: (0, d)),
            pl.BlockSpec((LANES, n_x), lambda b, d, c: (0, 0)),
        ],
        out_specs=pl.BlockSpec((None, L, n_x), lambda b, d, c: (d, chunk(b, d, c), 0)),
        out_shape=jax.ShapeDtypeStruct((2, t, n_x), BF16),
        scratch_shapes=[pltpu.VMEM((SSM_STATE, n_x), F32)],
        compiler_params=_params("parallel", "parallel", "arbitrary"),
        name="ssd",
    )(xbc_act, dt_raw, par, expand)


def _attn_kernel(q_ref, k_ref, v_ref, lam_ref, nw_ref, o_ref, m_ref, l_ref, acc_ref, *, n_kv):
    kv = pl.program_id(3)

    @pl.when(kv == 0)
    def _init():
        m_ref[...] = jnp.full_like(m_ref, -jnp.inf)
        l_ref[...] = jnp.zeros_like(l_ref)
        acc_ref[...] = jnp.zeros_like(acc_ref)

    v = v_ref[...]
    for mi in range(2):
        q = q_ref[:, mi * ATT_HEAD_DIM:(mi + 1) * ATT_HEAD_DIM]
        k = k_ref[:, mi * ATT_HEAD_DIM:(mi + 1) * ATT_HEAD_DIM]
        s = lax.dot_general(q, k, (((1,), (1,)), ((), ())), preferred_element_type=F32)
        m_prev = m_ref[mi]
        m_new = jnp.maximum(m_prev, jnp.max(s, axis=1, keepdims=True))
        alpha = jnp.exp2(m_prev - m_new)
        p = jnp.exp2(s - m_new)
        l_ref[mi] = alpha * l_ref[mi] + jnp.sum(p, axis=1, keepdims=True)
        acc_ref[mi] = alpha * acc_ref[mi] + jnp.dot(p.astype(BF16), v, preferred_element_type=F32)
        m_ref[mi] = m_new

    @pl.when(kv == n_kv - 1)
    def _fin():
        lam = (jnp.exp(jnp.sum(lam_ref[0:1, :] * lam_ref[1:2, :], axis=1, keepdims=True))
               - jnp.exp(jnp.sum(lam_ref[2:3, :] * lam_ref[3:4, :], axis=1, keepdims=True)) + LAMBDA_INIT)
        o = acc_ref[0] / l_ref[0] - lam * (acc_ref[1] / l_ref[1])
        o = o * lax.rsqrt(jnp.mean(o * o, axis=1, keepdims=True) + NORM_EPS)
        o_ref[...] = (o * nw_ref[...] * (1.0 - LAMBDA_INIT)).astype(o_ref.dtype)


def _attention(main, lam_par, norm_w, *, batch, seq, tq=1024, tk=512):
    t = main.shape[0]
    tq = min(tq, seq)
    tk = min(tk, seq)
    nq = seq // tq
    nk = seq // tk
    vd = ATT_V_DIM
    return pl.pallas_call(
        functools.partial(_attn_kernel, n_kv=nk),
        grid=(batch, ATT_HEADS, nq, nk),
        in_specs=[
            pl.BlockSpec((tq, vd), lambda b, h, i, j: (b * nq + i, COL_Q // vd + h)),
            pl.BlockSpec((tk, vd), lambda b, h, i, j: (b * nk + j, COL_K // vd + h)),
            pl.BlockSpec((tk, vd), lambda b, h, i, j: (b * nk + j, COL_V // vd + h)),
            pl.BlockSpec((8, ATT_HEAD_DIM), lambda b, h, i, j: (0, 0)),
            pl.BlockSpec((1, vd), lambda b, h, i, j: (0, 0)),
        ],
        out_specs=pl.BlockSpec((tq, vd), lambda b, h, i, j: (b * nq + i, h)),
        out_shape=jax.ShapeDtypeStruct((t, ATT_HEADS * vd), BF16),
        scratch_shapes=[
            pltpu.VMEM((2, tq, 1), F32),
            pltpu.VMEM((2, tq, 1), F32),
            pltpu.VMEM((2, tq, vd), F32),
        ],
        compiler_params=_params("parallel", "parallel", "parallel", "arbitrary"),
        name="diff_attention",
    )(main, main, main, lam_par, norm_w)


def _ssm_post_kernel(yf_ref, yb_ref, x_ref, z_ref, gs_ref, d_ref, nw_ref, w_ref, o_ref):
    y = yf_ref[...].astype(F32) + yb_ref[...].astype(F32) + d_ref[...] * x_ref[...].astype(F32)
    z = z_ref[...].astype(F32)
    y = y * (z * _sigmoid(z))
    gsz = y.shape[1] // SSM_GROUPS
    parts = []
    for g in range(SSM_GROUPS):
        seg = y[:, g * gsz:(g + 1) * gsz]
        parts.append(seg * lax.rsqrt(jnp.mean(seg * seg, axis=1, keepdims=True) + NORM_EPS))
    yn = jnp.concatenate(parts, axis=1) * nw_ref[...]
    ys = jnp.dot(yn.astype(BF16), w_ref[...], preferred_element_type=F32)
    o_ref[...] = _sigmoid(gs_ref[...].astype(F32)) * ys


def _ssm_post(y2, xbc_act, main, d_x, norm_w, w_br, *, tm=512):
    t = xbc_act.shape[0]
    tm = min(tm, t)
    d = D_MODEL
    row = lambda i: (i, 0)
    return pl.pallas_call(
        _ssm_post_kernel,
        grid=(t // tm,),
        in_specs=[
            pl.BlockSpec((None, tm, d), lambda i: (0, i, 0)),
            pl.BlockSpec((None, tm, d), lambda i: (1, i, 0)),
            pl.BlockSpec((tm, d), row),
            pl.BlockSpec((tm, d), lambda i: (i, COL_Z // d)),
            pl.BlockSpec((tm, d), lambda i: (i, COL_GS // d)),
            pl.BlockSpec((1, d), lambda i: (0, 0)),
            pl.BlockSpec((1, d), lambda i: (0, 0)),
            pl.BlockSpec((d, d), lambda i: (0, 0)),
        ],
        out_specs=pl.BlockSpec((tm, d), row),
        out_shape=jax.ShapeDtypeStruct((t, d), F32),
        compiler_params=_params("parallel"),
        name="ssm_post",
    )(y2, y2, xbc_act, main, main, d_x, norm_w, w_br)


def _mix_ln_kernel(p1_ref, oa_ref, ga_ref, h_ref, wa_ref, wo_ref, g_ref, b_ref, o_ref, ob_ref):
    ya = jnp.dot(oa_ref[...], wa_ref[...], preferred_element_type=F32)
    merged = p1_ref[...] + _sigmoid(ga_ref[...].astype(F32)) * ya
    mix = jnp.dot(merged.astype(BF16), wo_ref[...], preferred_element_type=F32)
    out = _layer_norm(ALPHA * h_ref[...] + mix, g_ref[...], b_ref[...])
    o_ref[...] = out
    ob_ref[...] = out.astype(BF16)


def _mix_ln(p1, o_att, main, h1, w_att_br, w_out, g, b, *, tm=256):
    t, d = h1.shape
    tm = min(tm, t)
    row = lambda i: (i, 0)
    const = lambda i: (0, 0)
    return pl.pallas_call(
        _mix_ln_kernel,
        grid=(t // tm,),
        in_specs=[
            pl.BlockSpec((tm, d), row),
            pl.BlockSpec((tm, d), row),
            pl.BlockSpec((tm, d), lambda i: (i, COL_GA // d)),
            pl.BlockSpec((tm, d), row),
            pl.BlockSpec((d, d), const),
            pl.BlockSpec((d, d), const),
            pl.BlockSpec((1, d), const),
            pl.BlockSpec((1, d), const),
        ],
        out_specs=[pl.BlockSpec((tm, d), row), pl.BlockSpec((tm, d), row)],
        out_shape=[jax.ShapeDtypeStruct((t, d), F32), jax.ShapeDtypeStruct((t, d), BF16)],
        compiler_params=_params("parallel"),
        name="mix_ln",
    )(p1, o_att, main, h1, w_att_br, w_out, g, b)


def _rope_tables(seq):
    half = ROT_DIM // 2
    pos = jnp.arange(seq, dtype=F32)
    inv_freq = ROPE_THETA ** (-jnp.arange(0, ROT_DIM, 2, dtype=F32) / ROT_DIM)
    ang = pos[:, None] * inv_freq[None, :]
    cos, sin = jnp.cos(ang), jnp.sin(ang)
    zeros = jnp.zeros((seq, LANES - ROT_DIM), F32)
    zh = jnp.zeros((seq, half), F32)
    cos_t = jnp.concatenate([cos, cos, jnp.ones((seq, LANES - ROT_DIM), F32)], axis=1)
    s1_t = jnp.concatenate([zh, sin, zeros], axis=1)
    s2_t = jnp.concatenate([-sin, zh, zeros], axis=1)
    return cos_t, s1_t, s2_t


def _row(v, n=None):
    v = v.astype(F32).reshape(1, -1)
    if n is not None and v.shape[1] < n:
        v = jnp.pad(v, ((0, 0), (0, n - v.shape[1])))
    return v


def _layer(x2, batch, seq, ffn1_w_gate, ffn1_w_up, ffn1_w_down, ln1_g, ln1_b, w_in, conv_w, conv_b,
           dt_bias_f, dt_bias_b, a_log_f, a_log_b, d_skip, ssm_norm_w, w_ssm_br,
           lambda_q1, lambda_k1, lambda_q2, lambda_k2, attn_norm_w, w_att_br, w_out,
           ln2_g, ln2_b, ffn2_w_gate, ffn2_w_up, ffn2_w_down, ln3_g, ln3_b):
    d = D_MODEL
    o_z, o_xbc, o_dtf, o_dtb = 0, d, d + CONV_DIM, d + CONV_DIM + SSM_HEADS
    o_q = o_dtb + SSM_HEADS
    o_k, o_v, o_gs, o_ga = o_q + d, o_q + 2 * d, o_q + 3 * d, o_q + 4 * d
    cols = lambda lo, n: w_in[:, lo:lo + n]
    w_main = jnp.concatenate(
        [cols(o_q, d), cols(o_k, d), cols(o_v, d), cols(o_z, d), cols(o_gs, d), cols(o_ga, d), cols(o_xbc, CONV_DIM)],
        axis=1).astype(BF16)
    pad = jnp.zeros((d, LANES - SSM_HEADS), w_in.dtype)
    w_dt = jnp.concatenate([cols(o_dtf, SSM_HEADS), pad, cols(o_dtb, SSM_HEADS), pad], axis=1).astype(BF16)

    h1, h1b = _ffn_ln(x2, ffn1_w_gate.astype(BF16), ffn1_w_up.astype(BF16), ffn1_w_down.astype(BF16),
                      _row(ln1_g), _row(ln1_b), emit_bf16=True)

    cos_t, s1_t, s2_t = _rope_tables(seq)
    main = _in_proj(h1b, w_main, cos_t, s1_t, s2_t, seq=seq)
    dt_raw = _dt_proj(h1b, w_dt)

    cw = jnp.pad(conv_w.astype(F32).reshape(CONV_WIDTH, CONV_DIM), ((0, 8 - CONV_WIDTH), (0, 0)))
    xbc_act = _conv_silu(main, cw, _row(conv_b), seq=seq)

    par = jnp.zeros((8, 2 * LANES), F32)
    par = par.at[0, 0:SSM_HEADS].set(dt_bias_f.astype(F32)).at[0, LANES:LANES + SSM_HEADS].set(dt_bias_b.astype(F32))
    par = par.at[1, 0:SSM_HEADS].set(a_log_f.astype(F32)).at[1, LANES:LANES + SSM_HEADS].set(a_log_b.astype(F32))
    n_x = SSM_HEADS * SSM_HEAD_DIM
    expand = (jnp.arange(LANES)[:, None] == (jnp.arange(n_x)[None, :] // SSM_HEAD_DIM)).astype(BF16)
    y2 = _ssd(xbc_act, dt_raw, par, expand, batch=batch, seq=seq)

    lam_par = jnp.zeros((8, ATT_HEAD_DIM), F32)
    for r, v in enumerate((lambda_q1, lambda_k1, lambda_q2, lambda_k2)):
        lam_par = lam_par.at[r].set(v.astype(F32))
    o_att = _attention(main, lam_par, _row(attn_norm_w), batch=batch, seq=seq)

    d_x = jnp.repeat(d_skip.astype(F32), SSM_HEAD_DIM).reshape(1, n_x)
    p1 = _ssm_post(y2, xbc_act, main, d_x, _row(ssm_norm_w), w_ssm_br.astype(BF16))
    h2, h2b = _mix_ln(p1, o_att, main, h1, w_att_br.astype(BF16), w_out.astype(BF16), _row(ln2_g), _row(ln2_b))
    del h2b
    (out,) = _ffn_ln(h2, ffn2_w_gate.astype(BF16), ffn2_w_up.astype(BF16), ffn2_w_down.astype(BF16),
                     _row(ln3_g), _row(ln3_b), emit_bf16=False)
    return out


def kernel(x, ffn1_w_gate, ffn1_w_up, ffn1_w_down, ln1_g, ln1_b, w_in, conv_w, conv_b, dt_bias_f, dt_bias_b, a_log_f, a_log_b, d_skip, ssm_norm_w, w_ssm_br, lambda_q1, lambda_k1, lambda_q2, lambda_k2, attn_norm_w, w_att_br, w_out, ln2_g, ln2_b, ffn2_w_gate, ffn2_w_up, ffn2_w_down, ln3_g, ln3_b):
    batch, seq, d = x.shape
    assert d == D_MODEL and seq % SSD_CHUNK == 0
    params = (ffn1_w_gate, ffn1_w_up, ffn1_w_down, ln1_g, ln1_b, w_in, conv_w, conv_b, dt_bias_f, dt_bias_b,
              a_log_f, a_log_b, d_skip, ssm_norm_w, w_ssm_br, lambda_q1, lambda_k1, lambda_q2, lambda_k2,
              attn_norm_w, w_att_br, w_out, ln2_g, ln2_b, ffn2_w_gate, ffn2_w_up, ffn2_w_down, ln3_g, ln3_b)
    assert all(p.shape[0] == DEPTH for p in params)
    out = _layer(x.reshape(batch * seq, d), batch, seq, *[p[0] for p in params])
    return out.reshape(batch, seq, d).astype(x.dtype)
```

```python
import functools
import math

import jax
import jax.numpy as jnp
from jax import lax
from jax.experimental import pallas as pl
from jax.experimental.pallas import tpu as pltpu

F32 = jnp.float32
BF16 = jnp.bfloat16

D_MODEL = 2048
SSM_HEAD_DIM = 64
SSM_HEADS = 32
SSM_GROUPS = 4
SSM_STATE = 128
CONV_WIDTH = 5
CONV_DIM = 3072
SSD_CHUNK = 128
ATT_HEADS = 8
ATT_HEAD_DIM = 128
ATT_V_DIM = 256
ROPE_THETA = 500000.0
ROT_DIM = 32
D_FF = 5632
DEPTH = 1
ALPHA = (2 * DEPTH) ** 0.25
NORM_EPS = 1e-5
LAMBDA_INIT = 0.8 - 0.6 * math.exp(-0.3 * 0)

LANES = 128
VMEM_LIMIT = 56 * 2 ** 20

COL_Q, COL_K, COL_V, COL_Z, COL_GS, COL_GA, COL_XBC = 0, 2048, 4096, 6144, 8192, 10240, 12288
N_MAIN = 15360
Q_SCALE = ATT_HEAD_DIM ** -0.5 * math.log2(math.e)


def _params(*sem):
    return pltpu.CompilerParams(dimension_semantics=sem, vmem_limit_bytes=VMEM_LIMIT)


def _layer_norm(y, g, b):
    mu = jnp.mean(y, axis=-1, keepdims=True)
    yc = y - mu
    var = jnp.mean(yc * yc, axis=-1, keepdims=True)
    return yc * lax.rsqrt(var + NORM_EPS) * g + b


def _sigmoid(v):
    return 1.0 / (1.0 + jnp.exp(-v))


def _ffn_ln_kernel(x_ref, wg_ref, wu_ref, wd_ref, g_ref, b_ref, *rest, n_f, emit_bf16):
    if emit_bf16:
        o_ref, ob_ref, xb_ref, acc_ref = rest
    else:
        o_ref, xb_ref, acc_ref = rest
    f = pl.program_id(1)

    @pl.when(f == 0)
    def _init():
        xb_ref[...] = x_ref[...].astype(BF16)
        acc_ref[...] = jnp.zeros_like(acc_ref)

    xb = xb_ref[...]
    hg = jnp.dot(xb, wg_ref[...], preferred_element_type=F32)
    hu = jnp.dot(xb, wu_ref[...], preferred_element_type=F32)
    h = hg * _sigmoid(hg) * hu
    acc_ref[...] += jnp.dot(h.astype(BF16), wd_ref[...], preferred_element_type=F32)

    @pl.when(f == n_f - 1)
    def _fin():
        y = ALPHA * x_ref[...] + 0.5 * acc_ref[...]
        out = _layer_norm(y, g_ref[...], b_ref[...])
        o_ref[...] = out
        if emit_bf16:
            ob_ref[...] = out.astype(BF16)


def _ffn_ln(x, wg, wu, wd, g, b, *, emit_bf16, tm=512, tf=512):
    t, d = x.shape
    dff = wg.shape[1]
    tm = min(tm, t)
    n_f = dff // tf
    out_shape = [jax.ShapeDtypeStruct((t, d), F32)]
    out_specs = [pl.BlockSpec((tm, d), lambda i, f: (i, 0))]
    if emit_bf16:
        out_shape.append(jax.ShapeDtypeStruct((t, d), BF16))
        out_specs.append(pl.BlockSpec((tm, d), lambda i, f: (i, 0)))
    return pl.pallas_call(
        functools.partial(_ffn_ln_kernel, n_f=n_f, emit_bf16=emit_bf16),
        grid=(t // tm, n_f),
        in_specs=[
            pl.BlockSpec((tm, d), lambda i, f: (i, 0)),
            pl.BlockSpec((d, tf), lambda i, f: (0, f)),
            pl.BlockSpec((d, tf), lambda i, f: (0, f)),
            pl.BlockSpec((tf, d), lambda i, f: (f, 0)),
            pl.BlockSpec((1, d), lambda i, f: (0, 0)),
            pl.BlockSpec((1, d), lambda i, f: (0, 0)),
        ],
        out_specs=out_specs,
        out_shape=out_shape,
        scratch_shapes=[pltpu.VMEM((tm, d), BF16), pltpu.VMEM((tm, d), F32)],
        compiler_params=_params("parallel", "arbitrary"),
        name="ffn_ln",
    )(x, wg, wu, wd, g, b)


def _inproj_kernel(x_ref, w_ref, cos_ref, s1_ref, s2_ref, o_ref, *, n_q, n_qk, tn):
    j = pl.program_id(1)
    y = jnp.dot(x_ref[...], w_ref[...], preferred_element_type=F32)

    @pl.when(j >= n_qk)
    def _plain():
        o_ref[...] = y.astype(o_ref.dtype)

    @pl.when(j < n_qk)
    def _rope():
        sc = jnp.where(j < n_q, Q_SCALE, 1.0).astype(F32)
        c = cos_ref[...] * sc
        s1 = s1_ref[...] * sc
        s2 = s2_ref[...] * sc
        half = ROT_DIM // 2
        for k in range(tn // LANES):
            yk = y[:, k * LANES:(k + 1) * LANES]
            r = yk * c + pltpu.roll(yk, half, 1) * s1 + pltpu.roll(yk, LANES - half, 1) * s2
            o_ref[:, k * LANES:(k + 1) * LANES] = r.astype(o_ref.dtype)


def _in_proj(hb, w_main, cos_t, s1_t, s2_t, *, seq, tm=1024, tn=512):
    t, d = hb.shape
    n = w_main.shape[1]
    tm = min(tm, seq)
    tiles_per_seq = seq // tm
    tab = pl.BlockSpec((tm, LANES), lambda i, j: (i % tiles_per_seq, 0))
    return pl.pallas_call(
        functools.partial(_inproj_kernel, n_q=COL_K // tn, n_qk=COL_V // tn, tn=tn),
        grid=(t // tm, n // tn),
        in_specs=[
            pl.BlockSpec((tm, d), lambda i, j: (i, 0)),
            pl.BlockSpec((d, tn), lambda i, j: (0, j)),
            tab, tab, tab,
        ],
        out_specs=pl.BlockSpec((tm, tn), lambda i, j: (i, j)),
        out_shape=jax.ShapeDtypeStruct((t, n), BF16),
        compiler_params=_params("parallel", "arbitrary"),
        name="in_proj",
    )(hb, w_main, cos_t, s1_t, s2_t)


def _dt_proj_kernel(x_ref, w_ref, o_ref):
    o_ref[...] = jnp.dot(x_ref[...], w_ref[...], preferred_element_type=F32)


def _dt_proj(hb, w_dt, *, tm=1024):
    t, d = hb.shape
    n = w_dt.shape[1]
    tm = min(tm, t)
    return pl.pallas_call(
        _dt_proj_kernel,
        grid=(t // tm,),
        in_specs=[pl.BlockSpec((tm, d), lambda i: (i, 0)), pl.BlockSpec((d, n), lambda i: (0, 0))],
        out_specs=pl.BlockSpec((tm, n), lambda i: (i, 0)),
        out_shape=jax.ShapeDtypeStruct((t, n), F32),
        compiler_params=_params("parallel"),
        name="dt_proj",
    )(hb, w_dt)


HALO = 16


def _conv_kernel(cur_ref, prev_ref, next_ref, w_ref, b_ref, o_ref, ext_ref, *, tm, tiles_per_seq):
    i = pl.program_id(0)
    first = (i % tiles_per_seq) == 0
    last = (i % tiles_per_seq) == tiles_per_seq - 1
    ext_ref[0:HALO, :] = jnp.where(first, 0.0, prev_ref[...].astype(F32))
    ext_ref[HALO:HALO + tm, :] = cur_ref[...].astype(F32)
    ext_ref[HALO + tm:2 * HALO + tm, :] = jnp.where(last, 0.0, next_ref[...].astype(F32))
    pad = CONV_WIDTH // 2
    acc = b_ref[...] + ext_ref[HALO - pad:HALO - pad + tm, :] * w_ref[0:1, :]
    for w in range(1, CONV_WIDTH):
        acc = acc + ext_ref[HALO - pad + w:HALO - pad + w + tm, :] * w_ref[w:w + 1, :]
    o_ref[...] = (acc * _sigmoid(acc)).astype(o_ref.dtype)


def _conv_silu(main, conv_w, conv_b, *, seq, tm=256):
    t = main.shape[0]
    tm = min(tm, seq)
    tiles_per_seq = seq // tm
    cb = COL_XBC // CONV_DIM
    r = tm // HALO
    nblk = t // HALO
    return pl.pallas_call(
        functools.partial(_conv_kernel, tm=tm, tiles_per_seq=tiles_per_seq),
        grid=(t // tm,),
        in_specs=[
            pl.BlockSpec((tm, CONV_DIM), lambda i: (i, cb)),
            pl.BlockSpec((HALO, CONV_DIM), lambda i: (jnp.maximum(i * r - 1, 0), cb)),
            pl.BlockSpec((HALO, CONV_DIM), lambda i: (jnp.minimum((i + 1) * r, nblk - 1), cb)),
            pl.BlockSpec((8, CONV_DIM), lambda i: (0, 0)),
            pl.BlockSpec((1, CONV_DIM), lambda i: (0, 0)),
        ],
        out_specs=pl.BlockSpec((tm, CONV_DIM), lambda i: (i, 0)),
        out_shape=jax.ShapeDtypeStruct((t, CONV_DIM), BF16),
        scratch_shapes=[pltpu.VMEM((tm + 2 * HALO, CONV_DIM), F32)],
        compiler_params=_params("parallel"),
        name="conv_silu",
    )(main, main, main, conv_w, conv_b)


def _split_dot(v, m01, terms):
    out = None
    r = v
    for _ in range(terms):
        part = r.astype(BF16)
        d = jnp.dot(part, m01, preferred_element_type=F32)
        out = d if out is None else out + d
        r = r - part.astype(F32)
    return out


def _ssd_kernel(xbc_ref, dt_ref, par_ref, e_ref, y_ref, h_ref):
    L = SSD_CHUNK
    d = pl.program_id(1)
    c = pl.program_id(2)

    @pl.when(c == 0)
    def _init():
        h_ref[...] = jnp.zeros_like(h_ref)

    fwd = d == 0
    pre = dt_ref[...] + par_ref[0:1, :]
    dt = jnp.maximum(pre, 0.0) + jnp.log(1.0 + jnp.exp(-jnp.abs(pre)))
    a = dt * (-jnp.exp(par_ref[1:2, :]))

    row = lax.broadcasted_iota(jnp.int32, (L, L), 0)
    col = lax.broadcasted_iota(jnp.int32, (L, L), 1)
    msk = (col - row) * (1 - 2 * d) <= 0
    tri = jnp.where(msk, 1.0, 0.0).astype(BF16)
    c_col = _tri_cumsum(tri, a)
    c_row = c_col.T
    tot = jnp.where(fwd, c_col[L - 1:L, :], c_col[0:1, :])
    e_out = jnp.exp(c_col)
    f_in = dt * jnp.exp(tot - c_col)
    cd = jnp.broadcast_to(jnp.exp(tot), (8, LANES))
    stacked = jnp.concatenate([dt, f_in, e_out, cd], axis=0)
    ex = _split_dot(stacked, e_ref[...], 2)
    dt_x = ex[0:L]
    f_x = ex[L:2 * L]
    e_x = ex[2 * L:3 * L]
    cd_x = ex[3 * L:3 * L + 1]

    n_x = SSM_HEADS * SSM_HEAD_DIM
    gn = SSM_GROUPS * SSM_STATE
    xf = xbc_ref[:, 0:n_x].astype(F32)
    xdt = (xf * dt_x).astype(BF16)
    xw = (xf * f_x).astype(BF16)
    lane = lax.broadcasted_iota(jnp.int32, (L, LANES), 1)
    gw = n_x // SSM_GROUPS
    for g in range(SSM_GROUPS):
        bg = xbc_ref[:, n_x + g * SSM_STATE:n_x + (g + 1) * SSM_STATE]
        cg = xbc_ref[:, n_x + gn + g * SSM_STATE:n_x + gn + (g + 1) * SSM_STATE]
        bgt = bg.astype(F32).T.astype(BF16)
        gmat = jnp.dot(cg, bgt, preferred_element_type=F32)
        hg = h_ref[:, g * gw:(g + 1) * gw]
        yoff = jnp.dot(cg, hg.astype(BF16), preferred_element_type=F32) * e_x[:, g * gw:(g + 1) * gw]
        st = jnp.dot(bgt, xw[:, g * gw:(g + 1) * gw], preferred_element_type=F32)
        h_ref[:, g * gw:(g + 1) * gw] = cd_x[:, g * gw:(g + 1) * gw] * hg + st
        for jp in range(gw // LANES):
            lo = g * gw + jp * LANES
            xp = xdt[:, lo:lo + LANES]
            outs = []
            for hh in range(2):
                h = lo // SSM_HEAD_DIM + hh
                seg = c_col[:, h:h + 1] - c_row[h:h + 1, :]
                decay = jnp.exp(jnp.where(msk, seg, -jnp.inf))
                m = (gmat * decay).astype(BF16)
                outs.append(jnp.dot(m, xp, preferred_element_type=F32))
            ydiag = jnp.where(lane < SSM_HEAD_DIM, outs[0], outs[1])
            y_ref[:, lo:lo + LANES] = (ydiag + yoff[:, jp * LANES:(jp + 1) * LANES]).astype(y_ref.dtype)


def _tri_cumsum(tri, a):
    out = None
    r = a
    for _ in range(3):
        part = r.astype(BF16)
        d = jnp.dot(tri, part, preferred_element_type=F32)
        out = d if out is None else out + d
        r = r - part.astype(F32)
    return out


def _ssd(xbc_act, dt_raw, par, expand, *, batch, seq):
    t = xbc_act.shape[0]
    L = SSD_CHUNK
    nc = seq // L
    n_x = SSM_HEADS * SSM_HEAD_DIM

    def chunk(b, d, c):
        return b * nc + c + d * (nc - 1 - 2 * c)

    return pl.pallas_call(
        _ssd_kernel,
        grid=(batch, 2, nc),
        in_specs=[
            pl.BlockSpec((L, CONV_DIM), lambda b, d, c: (chunk(b, d, c), 0)),
            pl.BlockSpec((L, LANES), lambda b, d, c: (chunk(b, d, c), d)),
            pl.BlockSpec((8, LANES), lambda b, d, c: (0, d)),
            pl.BlockSpec((LANES, n_x), lambda b, d, c: (0, 0)),
        ],
        out_specs=pl.BlockSpec((None, L, n_x), lambda b, d, c: (d, chunk(b, d, c), 0)),
        out_shape=jax.ShapeDtypeStruct((2, t, n_x), BF16),
        scratch_shapes=[pltpu.VMEM((SSM_STATE, n_x), F32)],
        compiler_params=_params("parallel", "parallel", "arbitrary"),
        name="ssd",
    )(xbc_act, dt_raw, par, expand)


def _attn_kernel(q_ref, k_ref, v_ref, lam_ref, nw_ref, o_ref, m_ref, l_ref, acc_ref, *, n_kv, tq, tk, rows):
    kv = pl.program_id(3)

    @pl.when(kv == 0)
    def _init():
        m_ref[...] = jnp.full_like(m_ref, -jnp.inf)
        l_ref[...] = jnp.zeros_like(l_ref)
        acc_ref[...] = jnp.zeros_like(acc_ref)

    v = v_ref[...]
    n_lt = tk // LANES
    for rb in range(tq // rows):
        r0 = rb * rows
        for mi in range(2):
            q = q_ref[r0:r0 + rows, mi * ATT_HEAD_DIM:(mi + 1) * ATT_HEAD_DIM]
            k = k_ref[:, mi * ATT_HEAD_DIM:(mi + 1) * ATT_HEAD_DIM]
            s = lax.dot_general(q, k, (((1,), (1,)), ((), ())), preferred_element_type=F32)
            pm = s[:, 0:LANES]
            for c in range(1, n_lt):
                pm = jnp.maximum(pm, s[:, c * LANES:(c + 1) * LANES])
            m_prev = m_ref[mi, r0:r0 + rows, :]
            m_new = jnp.maximum(m_prev, jnp.max(pm, axis=1, keepdims=True))
            alpha = jnp.exp2(m_prev - m_new)
            lsum = None
            ps = []
            for c in range(n_lt):
                pc = jnp.exp2(s[:, c * LANES:(c + 1) * LANES] - m_new)
                lsum = pc if lsum is None else lsum + pc
                ps.append(pc.astype(BF16))
            p = jnp.concatenate(ps, axis=1)
            l_ref[mi, r0:r0 + rows, :] = alpha * l_ref[mi, r0:r0 + rows, :] + lsum
            pv = jnp.dot(p, v, preferred_element_type=F32)
            alpha2 = jnp.concatenate([alpha] * (ATT_V_DIM // LANES), axis=1)
            acc_ref[mi, r0:r0 + rows, :] = alpha2 * acc_ref[mi, r0:r0 + rows, :] + pv
            m_ref[mi, r0:r0 + rows, :] = m_new

    @pl.when(kv == n_kv - 1)
    def _fin():
        lam = (jnp.exp(jnp.sum(lam_ref[0:1, :] * lam_ref[1:2, :], axis=1, keepdims=True))
               - jnp.exp(jnp.sum(lam_ref[2:3, :] * lam_ref[3:4, :], axis=1, keepdims=True)) + LAMBDA_INIT)
        l0 = jnp.sum(l_ref[0], axis=1, keepdims=True)
        l1 = jnp.sum(l_ref[1], axis=1, keepdims=True)
        o = acc_ref[0] / l0 - lam * (acc_ref[1] / l1)
        o = o * lax.rsqrt(jnp.mean(o * o, axis=1, keepdims=True) + NORM_EPS)
        o_ref[...] = (o * nw_ref[...] * (1.0 - LAMBDA_INIT)).astype(o_ref.dtype)


def _attention(main, lam_par, norm_w, *, batch, seq, tq=1024, tk=1024, rows=256):
    t = main.shape[0]
    tq = min(tq, seq)
    tk = min(tk, seq)
    rows = min(rows, tq)
    nq = seq // tq
    nk = seq // tk
    vd = ATT_V_DIM
    return pl.pallas_call(
        functools.partial(_attn_kernel, n_kv=nk, tq=tq, tk=tk, rows=rows),
        grid=(batch, ATT_HEADS, nq, nk),
        in_specs=[
            pl.BlockSpec((tq, vd), lambda b, h, i, j: (b * nq + i, COL_Q // vd + h)),
            pl.BlockSpec((tk, vd), lambda b, h, i, j: (b * nk + j, COL_K // vd + h)),
            pl.BlockSpec((tk, vd), lambda b, h, i, j: (b * nk + j, COL_V // vd + h)),
            pl.BlockSpec((8, ATT_HEAD_DIM), lambda b, h, i, j: (0, 0)),
            pl.BlockSpec((1, vd), lambda b, h, i, j: (0, 0)),
        ],
        out_specs=pl.BlockSpec((tq, vd), lambda b, h, i, j: (b * nq + i, h)),
        out_shape=jax.ShapeDtypeStruct((t, ATT_HEADS * vd), BF16),
        scratch_shapes=[
            pltpu.VMEM((2, tq, LANES), F32),
            pltpu.VMEM((2, tq, LANES), F32),
            pltpu.VMEM((2, tq, vd), F32),
        ],
        compiler_params=_params("parallel", "parallel", "parallel", "arbitrary"),
        name="diff_attention",
    )(main, main, main, lam_par, norm_w)


def _ssm_post_kernel(yf_ref, yb_ref, x_ref, z_ref, gs_ref, d_ref, nw_ref, w_ref, o_ref):
    y = yf_ref[...].astype(F32) + yb_ref[...].astype(F32) + d_ref[...] * x_ref[...].astype(F32)
    z = z_ref[...].astype(F32)
    y = y * (z * _sigmoid(z))
    gsz = y.shape[1] // SSM_GROUPS
    parts = []
    for g in range(SSM_GROUPS):
        seg = y[:, g * gsz:(g + 1) * gsz]
        parts.append(seg * lax.rsqrt(jnp.mean(seg * seg, axis=1, keepdims=True) + NORM_EPS))
    yn = jnp.concatenate(parts, axis=1) * nw_ref[...]
    ys = jnp.dot(yn.astype(BF16), w_ref[...], preferred_element_type=F32)
    o_ref[...] = _sigmoid(gs_ref[...].astype(F32)) * ys


def _ssm_post(y2, xbc_act, main, d_x, norm_w, w_br, *, tm=512):
    t = xbc_act.shape[0]
    tm = min(tm, t)
    d = D_MODEL
    row = lambda i: (i, 0)
    return pl.pallas_call(
        _ssm_post_kernel,
        grid=(t // tm,),
        in_specs=[
            pl.BlockSpec((None, tm, d), lambda i: (0, i, 0)),
            pl.BlockSpec((None, tm, d), lambda i: (1, i, 0)),
            pl.BlockSpec((tm, d), row),
            pl.BlockSpec((tm, d), lambda i: (i, COL_Z // d)),
            pl.BlockSpec((tm, d), lambda i: (i, COL_GS // d)),
            pl.BlockSpec((1, d), lambda i: (0, 0)),
            pl.BlockSpec((1, d), lambda i: (0, 0)),
            pl.BlockSpec((d, d), lambda i: (0, 0)),
        ],
        out_specs=pl.BlockSpec((tm, d), row),
        out_shape=jax.ShapeDtypeStruct((t, d), F32),
        compiler_params=_params("parallel"),
        name="ssm_post",
    )(y2, y2, xbc_act, main, main, d_x, norm_w, w_br)


def _mix_ln_kernel(p1_ref, oa_ref, ga_ref, h_ref, wa_ref, wo_ref, g_ref, b_ref, o_ref, ob_ref):
    ya = jnp.dot(oa_ref[...], wa_ref[...], preferred_element_type=F32)
    merged = p1_ref[...] + _sigmoid(ga_ref[...].astype(F32)) * ya
    mix = jnp.dot(merged.astype(BF16), wo_ref[...], preferred_element_type=F32)
    out = _layer_norm(ALPHA * h_ref[...] + mix, g_ref[...], b_ref[...])
    o_ref[...] = out
    ob_ref[...] = out.astype(BF16)


def _mix_ln(p1, o_att, main, h1, w_att_br, w_out, g, b, *, tm=256):
    t, d = h1.shape
    tm = min(tm, t)
    row = lambda i: (i, 0)
    const = lambda i: (0, 0)
    return pl.pallas_call(
        _mix_ln_kernel,
        grid=(t // tm,),
        in_specs=[
            pl.BlockSpec((tm, d), row),
            pl.BlockSpec((tm, d), row),
            pl.BlockSpec((tm, d), lambda i: (i, COL_GA // d)),
            pl.BlockSpec((tm, d), row),
            pl.BlockSpec((d, d), const),
            pl.BlockSpec((d, d), const),
            pl.BlockSpec((1, d), const),
            pl.BlockSpec((1, d), const),
        ],
        out_specs=[pl.BlockSpec((tm, d), row), pl.BlockSpec((tm, d), row)],
        out_shape=[jax.ShapeDtypeStruct((t, d), F32), jax.ShapeDtypeStruct((t, d), BF16)],
        compiler_params=_params("parallel"),
        name="mix_ln",
    )(p1, o_att, main, h1, w_att_br, w_out, g, b)


def _rope_tables(seq):
    half = ROT_DIM // 2
    pos = jnp.arange(seq, dtype=F32)
    inv_freq = ROPE_THETA ** (-jnp.arange(0, ROT_DIM, 2, dtype=F32) / ROT_DIM)
    ang = pos[:, None] * inv_freq[None, :]
    cos, sin = jnp.cos(ang), jnp.sin(ang)
    zeros = jnp.zeros((seq, LANES - ROT_DIM), F32)
    zh = jnp.zeros((seq, half), F32)
    cos_t = jnp.concatenate([cos, cos, jnp.ones((seq, LANES - ROT_DIM), F32)], axis=1)
    s1_t = jnp.concatenate([zh, sin, zeros], axis=1)
    s2_t = jnp.concatenate([-sin, zh, zeros], axis=1)
    return cos_t, s1_t, s2_t


def _row(v, n=None):
    v = v.astype(F32).reshape(1, -1)
    if n is not None and v.shape[1] < n:
        v = jnp.pad(v, ((0, 0), (0, n - v.shape[1])))
    return v


def _layer(x2, batch, seq, ffn1_w_gate, ffn1_w_up, ffn1_w_down, ln1_g, ln1_b, w_in, conv_w, conv_b,
           dt_bias_f, dt_bias_b, a_log_f, a_log_b, d_skip, ssm_norm_w, w_ssm_br,
           lambda_q1, lambda_k1, lambda_q2, lambda_k2, attn_norm_w, w_att_br, w_out,
           ln2_g, ln2_b, ffn2_w_gate, ffn2_w_up, ffn2_w_down, ln3_g, ln3_b):
    d = D_MODEL
    o_z, o_xbc, o_dtf, o_dtb = 0, d, d + CONV_DIM, d + CONV_DIM + SSM_HEADS
    o_q = o_dtb + SSM_HEADS
    o_k, o_v, o_gs, o_ga = o_q + d, o_q + 2 * d, o_q + 3 * d, o_q + 4 * d
    cols = lambda lo, n: w_in[:, lo:lo + n]
    w_main = jnp.concatenate(
        [cols(o_q, d), cols(o_k, d), cols(o_v, d), cols(o_z, d), cols(o_gs, d), cols(o_ga, d), cols(o_xbc, CONV_DIM)],
        axis=1).astype(BF16)
    pad = jnp.zeros((d, LANES - SSM_HEADS), w_in.dtype)
    w_dt = jnp.concatenate([cols(o_dtf, SSM_HEADS), pad, cols(o_dtb, SSM_HEADS), pad], axis=1).astype(BF16)

    h1, h1b = _ffn_ln(x2, ffn1_w_gate.astype(BF16), ffn1_w_up.astype(BF16), ffn1_w_down.astype(BF16),
                      _row(ln1_g), _row(ln1_b), emit_bf16=True)

    cos_t, s1_t, s2_t = _rope_tables(seq)
    main = _in_proj(h1b, w_main, cos_t, s1_t, s2_t, seq=seq)
    dt_raw = _dt_proj(h1b, w_dt)

    cw = jnp.pad(conv_w.astype(F32).reshape(CONV_WIDTH, CONV_DIM), ((0, 8 - CONV_WIDTH), (0, 0)))
    xbc_act = _conv_silu(main, cw, _row(conv_b), seq=seq)

    par = jnp.zeros((8, 2 * LANES), F32)
    par = par.at[0, 0:SSM_HEADS].set(dt_bias_f.astype(F32)).at[0, LANES:LANES + SSM_HEADS].set(dt_bias_b.astype(F32))
    par = par.at[1, 0:SSM_HEADS].set(a_log_f.astype(F32)).at[1, LANES:LANES + SSM_HEADS].set(a_log_b.astype(F32))
    n_x = SSM_HEADS * SSM_HEAD_DIM
    expand = (jnp.arange(LANES)[:, None] == (jnp.arange(n_x)[None, :] // SSM_HEAD_DIM)).astype(BF16)
    y2 = _ssd(xbc_act, dt_raw, par, expand, batch=batch, seq=seq)

    lam_par = jnp.zeros((8, ATT_HEAD_DIM), F32)
    for r, v in enumerate((lambda_q1, lambda_k1, lambda_q2, lambda_k2)):
        lam_par = lam_par.at[r].set(v.astype(F32))
    o_att = _attention(main, lam_par, _row(attn_norm_w), batch=batch, seq=seq)

    d_x = jnp.repeat(d_skip.astype(F32), SSM_HEAD_DIM).reshape(1, n_x)
    p1 = _ssm_post(y2, xbc_act, main, d_x, _row(ssm_norm_w), w_ssm_br.astype(BF16))
    h2, h2b = _mix_ln(p1, o_att, main, h1, w_att_br.astype(BF16), w_out.astype(BF16), _row(ln2_g), _row(ln2_b))
    del h2b
    (out,) = _ffn_ln(h2, ffn2_w_gate.astype(BF16), ffn2_w_up.astype(BF16), ffn2_w_down.astype(BF16),
                     _row(ln3_g), _row(ln3_b), emit_bf16=False)
    return out


def kernel(x, ffn1_w_gate, ffn1_w_up, ffn1_w_down, ln1_g, ln1_b, w_in, conv_w, conv_b, dt_bias_f, dt_bias_b, a_log_f, a_log_b, d_skip, ssm_norm_w, w_ssm_br, lambda_q1, lambda_k1, lambda_q2, lambda_k2, attn_norm_w, w_att_br, w_out, ln2_g, ln2_b, ffn2_w_gate, ffn2_w_up, ffn2_w_down, ln3_g, ln3_b):
    batch, seq, d = x.shape
    assert d == D_MODEL and seq % SSD_CHUNK == 0
    params = (ffn1_w_gate, ffn1_w_up, ffn1_w_down, ln1_g, ln1_b, w_in, conv_w, conv_b, dt_bias_f, dt_bias_b,
              a_log_f, a_log_b, d_skip, ssm_norm_w, w_ssm_br, lambda_q1, lambda_k1, lambda_q2, lambda_k2,
              attn_norm_w, w_att_br, w_out, ln2_g, ln2_b, ffn2_w_gate, ffn2_w_up, ffn2_w_down, ln3_g, ln3_b)
    assert all(p.shape[0] == DEPTH for p in params)
    out = _layer(x.reshape(batch * seq, d), batch, seq, *[p[0] for p in params])
    return out.reshape(batch, seq, d).astype(x.dtype)
```

```python
import functools
import math

import jax
import jax.numpy as jnp
from jax import lax
from jax.experimental import pallas as pl
from jax.experimental.pallas import tpu as pltpu

F32 = jnp.float32
BF16 = jnp.bfloat16

D_MODEL = 2048
SSM_HEAD_DIM = 64
SSM_HEADS = 32
SSM_GROUPS = 4
SSM_STATE = 128
CONV_WIDTH = 5
CONV_DIM = 3072
SSD_CHUNK = 128
ATT_HEADS = 8
ATT_HEAD_DIM = 128
ATT_V_DIM = 256
ROPE_THETA = 500000.0
ROT_DIM = 32
D_FF = 5632
DEPTH = 1
ALPHA = (2 * DEPTH) ** 0.25
NORM_EPS = 1e-5
LAMBDA_INIT = 0.8 - 0.6 * math.exp(-0.3 * 0)

LANES = 128
VMEM_LIMIT = 56 * 2 ** 20

COL_Q, COL_K, COL_V, COL_Z, COL_GS, COL_GA, COL_XBC = 0, 2048, 4096, 6144, 8192, 10240, 12288
N_MAIN = 15360
Q_SCALE = ATT_HEAD_DIM ** -0.5 * math.log2(math.e)


def _params(*sem):
    return pltpu.CompilerParams(dimension_semantics=sem, vmem_limit_bytes=VMEM_LIMIT)


def _layer_norm(y, g, b):
    mu = jnp.mean(y, axis=-1, keepdims=True)
    yc = y - mu
    var = jnp.mean(yc * yc, axis=-1, keepdims=True)
    return yc * lax.rsqrt(var + NORM_EPS) * g + b


def _sigmoid(v):
    return 1.0 / (1.0 + jnp.exp(-v))


def _ffn_ln_kernel(x_ref, wg_ref, wu_ref, wd_ref, g_ref, b_ref, *rest, n_f, emit_bf16):
    if emit_bf16:
        o_ref, ob_ref, xb_ref, acc_ref = rest
    else:
        o_ref, xb_ref, acc_ref = rest
    f = pl.program_id(1)

    @pl.when(f == 0)
    def _init():
        xb_ref[...] = x_ref[...].astype(BF16)
        acc_ref[...] = jnp.zeros_like(acc_ref)

    xb = xb_ref[...]
    hg = jnp.dot(xb, wg_ref[...], preferred_element_type=F32)
    hu = jnp.dot(xb, wu_ref[...], preferred_element_type=F32)
    h = hg * _sigmoid(hg) * hu
    acc_ref[...] += jnp.dot(h.astype(BF16), wd_ref[...], preferred_element_type=F32)

    @pl.when(f == n_f - 1)
    def _fin():
        y = ALPHA * x_ref[...] + 0.5 * acc_ref[...]
        out = _layer_norm(y, g_ref[...], b_ref[...])
        o_ref[...] = out
        if emit_bf16:
            ob_ref[...] = out.astype(BF16)


def _ffn_ln(x, wg, wu, wd, g, b, *, emit_bf16, tm=512, tf=512):
    t, d = x.shape
    dff = wg.shape[1]
    tm = min(tm, t)
    n_f = dff // tf
    out_shape = [jax.ShapeDtypeStruct((t, d), F32)]
    out_specs = [pl.BlockSpec((tm, d), lambda i, f: (i, 0))]
    if emit_bf16:
        out_shape.append(jax.ShapeDtypeStruct((t, d), BF16))
        out_specs.append(pl.BlockSpec((tm, d), lambda i, f: (i, 0)))
    return pl.pallas_call(
        functools.partial(_ffn_ln_kernel, n_f=n_f, emit_bf16=emit_bf16),
        grid=(t // tm, n_f),
        in_specs=[
            pl.BlockSpec((tm, d), lambda i, f: (i, 0)),
            pl.BlockSpec((d, tf), lambda i, f: (0, f)),
            pl.BlockSpec((d, tf), lambda i, f: (0, f)),
            pl.BlockSpec((tf, d), lambda i, f: (f, 0)),
            pl.BlockSpec((1, d), lambda i, f: (0, 0)),
            pl.BlockSpec((1, d), lambda i, f: (0, 0)),
        ],
        out_specs=out_specs,
        out_shape=out_shape,
        scratch_shapes=[pltpu.VMEM((tm, d), BF16), pltpu.VMEM((tm, d), F32)],
        compiler_params=_params("parallel", "arbitrary"),
        name="ffn_ln",
    )(x, wg, wu, wd, g, b)


def _inproj_kernel(x_ref, w_ref, cos_ref, s1_ref, s2_ref, o_ref, *, n_q, n_qk, tn):
    j = pl.program_id(1)
    y = jnp.dot(x_ref[...], w_ref[...], preferred_element_type=F32)

    @pl.when(j >= n_qk)
    def _plain():
        o_ref[...] = y.astype(o_ref.dtype)

    @pl.when(j < n_qk)
    def _rope():
        sc = jnp.where(j < n_q, Q_SCALE, 1.0).astype(F32)
        c = cos_ref[...] * sc
        s1 = s1_ref[...] * sc
        s2 = s2_ref[...] * sc
        half = ROT_DIM // 2
        for k in range(tn // LANES):
            yk = y[:, k * LANES:(k + 1) * LANES]
            r = yk * c + pltpu.roll(yk, half, 1) * s1 + pltpu.roll(yk, LANES - half, 1) * s2
            o_ref[:, k * LANES:(k + 1) * LANES] = r.astype(o_ref.dtype)


def _in_proj(hb, w_main, cos_t, s1_t, s2_t, *, seq, tm=1024, tn=512):
    t, d = hb.shape
    n = w_main.shape[1]
    tm = min(tm, seq)
    tiles_per_seq = seq // tm
    tab = pl.BlockSpec((tm, LANES), lambda i, j: (i % tiles_per_seq, 0))
    return pl.pallas_call(
        functools.partial(_inproj_kernel, n_q=COL_K // tn, n_qk=COL_V // tn, tn=tn),
        grid=(t // tm, n // tn),
        in_specs=[
            pl.BlockSpec((tm, d), lambda i, j: (i, 0)),
            pl.BlockSpec((d, tn), lambda i, j: (0, j)),
            tab, tab, tab,
        ],
        out_specs=pl.BlockSpec((tm, tn), lambda i, j: (i, j)),
        out_shape=jax.ShapeDtypeStruct((t, n), BF16),
        compiler_params=_params("parallel", "arbitrary"),
        name="in_proj",
    )(hb, w_main, cos_t, s1_t, s2_t)


def _dt_proj_kernel(x_ref, w_ref, o_ref):
    o_ref[...] = jnp.dot(x_ref[...], w_ref[...], preferred_element_type=F32)


def _dt_proj(hb, w_dt, *, tm=1024):
    t, d = hb.shape
    n = w_dt.shape[1]
    tm = min(tm, t)
    return pl.pallas_call(
        _dt_proj_kernel,
        grid=(t // tm,),
        in_specs=[pl.BlockSpec((tm, d), lambda i: (i, 0)), pl.BlockSpec((d, n), lambda i: (0, 0))],
        out_specs=pl.BlockSpec((tm, n), lambda i: (i, 0)),
        out_shape=jax.ShapeDtypeStruct((t, n), F32),
        compiler_params=_params("parallel"),
        name="dt_proj",
    )(hb, w_dt)


HALO = 16


def _conv_kernel(cur_ref, prev_ref, next_ref, w_ref, b_ref, o_ref, ext_ref, *, tm, tiles_per_seq):
    i = pl.program_id(0)
    first = (i % tiles_per_seq) == 0
    last = (i % tiles_per_seq) == tiles_per_seq - 1
    ext_ref[0:HALO, :] = jnp.where(first, 0.0, prev_ref[...].astype(F32))
    ext_ref[HALO:HALO + tm, :] = cur_ref[...].astype(F32)
    ext_ref[HALO + tm:2 * HALO + tm, :] = jnp.where(last, 0.0, next_ref[...].astype(F32))
    pad = CONV_WIDTH // 2
    acc = b_ref[...] + ext_ref[HALO - pad:HALO - pad + tm, :] * w_ref[0:1, :]
    for w in range(1, CONV_WIDTH):
        acc = acc + ext_ref[HALO - pad + w:HALO - pad + w + tm, :] * w_ref[w:w + 1, :]
    o_ref[...] = (acc * _sigmoid(acc)).astype(o_ref.dtype)


def _conv_silu(main, conv_w, conv_b, *, seq, tm=256):
    t = main.shape[0]
    tm = min(tm, seq)
    tiles_per_seq = seq // tm
    cb = COL_XBC // CONV_DIM
    r = tm // HALO
    nblk = t // HALO
    return pl.pallas_call(
        functools.partial(_conv_kernel, tm=tm, tiles_per_seq=tiles_per_seq),
        grid=(t // tm,),
        in_specs=[
            pl.BlockSpec((tm, CONV_DIM), lambda i: (i, cb)),
            pl.BlockSpec((HALO, CONV_DIM), lambda i: (jnp.maximum(i * r - 1, 0), cb)),
            pl.BlockSpec((HALO, CONV_DIM), lambda i: (jnp.minimum((i + 1) * r, nblk - 1), cb)),
            pl.BlockSpec((8, CONV_DIM), lambda i: (0, 0)),
            pl.BlockSpec((1, CONV_DIM), lambda i: (0, 0)),
        ],
        out_specs=pl.BlockSpec((tm, CONV_DIM), lambda i: (i, 0)),
        out_shape=jax.ShapeDtypeStruct((t, CONV_DIM), BF16),
        scratch_shapes=[pltpu.VMEM((tm + 2 * HALO, CONV_DIM), F32)],
        compiler_params=_params("parallel"),
        name="conv_silu",
    )(main, main, main, conv_w, conv_b)


def _split_dot(v, m01, terms):
    out = None
    r = v
    for _ in range(terms):
        part = r.astype(BF16)
        d = jnp.dot(part, m01, preferred_element_type=F32)
        out = d if out is None else out + d
        r = r - part.astype(F32)
    return out


def _ssd_kernel(xbc_ref, dt_ref, par_ref, e_ref, y_ref, h_ref):
    L = SSD_CHUNK
    d = pl.program_id(1)
    c = pl.program_id(2)

    @pl.when(c == 0)
    def _init():
        h_ref[...] = jnp.zeros_like(h_ref)

    fwd = d == 0
    pre = dt_ref[...] + par_ref[0:1, :]
    dt = jnp.maximum(pre, 0.0) + jnp.log(1.0 + jnp.exp(-jnp.abs(pre)))
    a = dt * (-jnp.exp(par_ref[1:2, :]))

    row = lax.broadcasted_iota(jnp.int32, (L, L), 0)
    col = lax.broadcasted_iota(jnp.int32, (L, L), 1)
    msk = (col - row) * (1 - 2 * d) <= 0
    tri = jnp.where(msk, 1.0, 0.0).astype(BF16)
    c_col = _tri_cumsum(tri, a)
    c_row = c_col.T
    tot = jnp.where(fwd, c_col[L - 1:L, :], c_col[0:1, :])
    e_out = jnp.exp(c_col)
    f_in = dt * jnp.exp(tot - c_col)
    cd = jnp.broadcast_to(jnp.exp(tot), (8, LANES))
    stacked = jnp.concatenate([dt, f_in, e_out, cd], axis=0)
    ex = _split_dot(stacked, e_ref[...], 2)
    dt_x = ex[0:L]
    f_x = ex[L:2 * L]
    e_x = ex[2 * L:3 * L]
    cd_x = ex[3 * L:3 * L + 1]

    n_x = SSM_HEADS * SSM_HEAD_DIM
    gn = SSM_GROUPS * SSM_STATE
    xf = xbc_ref[:, 0:n_x].astype(F32)
    xdt = (xf * dt_x).astype(BF16)
    xw = (xf * f_x).astype(BF16)
    lane = lax.broadcasted_iota(jnp.int32, (L, LANES), 1)
    gw = n_x // SSM_GROUPS
    for g in range(SSM_GROUPS):
        bg = xbc_ref[:, n_x + g * SSM_STATE:n_x + (g + 1) * SSM_STATE]
        cg = xbc_ref[:, n_x + gn + g * SSM_STATE:n_x + gn + (g + 1) * SSM_STATE]
        bgt = bg.astype(F32).T.astype(BF16)
        gmat = jnp.dot(cg, bgt, preferred_element_type=F32)
        hg = h_ref[:, g * gw:(g + 1) * gw]
        yoff = jnp.dot(cg, hg.astype(BF16), preferred_element_type=F32) * e_x[:, g * gw:(g + 1) * gw]
        st = jnp.dot(bgt, xw[:, g * gw:(g + 1) * gw], preferred_element_type=F32)
        h_ref[:, g * gw:(g + 1) * gw] = cd_x[:, g * gw:(g + 1) * gw] * hg + st
        for jp in range(gw // LANES):
            lo = g * gw + jp * LANES
            xp = xdt[:, lo:lo + LANES]
            outs = []
            for hh in range(2):
                h = lo // SSM_HEAD_DIM + hh
                seg = c_col[:, h:h + 1] - c_row[h:h + 1, :]
                decay = jnp.exp(jnp.where(msk, seg, -jnp.inf))
                m = (gmat * decay).astype(BF16)
                outs.append(jnp.dot(m, xp, preferred_element_type=F32))
            ydiag = jnp.where(lane < SSM_HEAD_DIM, outs[0], outs[1])
            y_ref[:, lo:lo + LANES] = (ydiag + yoff[:, jp * LANES:(jp + 1) * LANES]).astype(y_ref.dtype)


def _tri_cumsum(tri, a):
    out = None
    r = a
    for _ in range(3):
        part = r.astype(BF16)
        d = jnp.dot(tri, part, preferred_element_type=F32)
        out = d if out is None else out + d
        r = r - part.astype(F32)
    return out


def _ssd(xbc_act, dt_raw, par, expand, *, batch, seq):
    t = xbc_act.shape[0]
    L = SSD_CHUNK
    nc = seq // L
    n_x = SSM_HEADS * SSM_HEAD_DIM

    def chunk(b, d, c):
        return b * nc + c + d * (nc - 1 - 2 * c)

    return pl.pallas_call(
        _ssd_kernel,
        grid=(batch, 2, nc),
        in_specs=[
            pl.BlockSpec((L, CONV_DIM), lambda b, d, c: (chunk(b, d, c), 0)),
            pl.BlockSpec((L, LANES), lambda b, d, c: (chunk(b, d, c), d)),
            pl.BlockSpec((8, LANES), lambda b, d, c: (0, d)),
            pl.BlockSpec((LANES, n_x), lambda b, d, c: (0, 0)),
        ],
        out_specs=pl.BlockSpec((None, L, n_x), lambda b, d, c: (d, chunk(b, d, c), 0)),
        out_shape=jax.ShapeDtypeStruct((2, t, n_x), BF16),
        scratch_shapes=[pltpu.VMEM((SSM_STATE, n_x), F32)],
        compiler_params=_params("parallel", "parallel", "arbitrary"),
        name="ssd",
    )(xbc_act, dt_raw, par, expand)


STALE_MAX_LIMIT = 64.0


def _attn_block(q_ref, k_ref, v_ref, st_in, st_out, gap_ref, *, stale, tq, tk, rows):
    m_in, l_in, acc_in = st_in
    m_out, l_out, acc_out = st_out
    if stale:
        m_bak, l_bak, acc_bak = st_out
        m_out, l_out, acc_out = st_in
    v = v_ref[...]
    n_lt = tk // LANES
    gap = None
    for rb in range(tq // rows):
        r0 = rb * rows
        for mi in range(2):
            q = q_ref[r0:r0 + rows, mi * ATT_HEAD_DIM:(mi + 1) * ATT_HEAD_DIM]
            k = k_ref[:, mi * ATT_HEAD_DIM:(mi + 1) * ATT_HEAD_DIM]
            s = lax.dot_general(q, k, (((1,), (1,)), ((), ())), preferred_element_type=F32)
            m_prev = m_in[mi, r0:r0 + rows, :]
            pm = s[:, 0:LANES]
            for c in range(1, n_lt):
                pm = jnp.maximum(pm, s[:, c * LANES:(c + 1) * LANES])
            m_new = jnp.maximum(m_prev, jnp.max(pm, axis=1, keepdims=True))
            alpha = jnp.exp2(m_prev - m_new)
            m_exp = m_prev if stale else m_new
            lsum = None
            ps = []
            for c in range(n_lt):
                pc = jnp.exp2(s[:, c * LANES:(c + 1) * LANES] - m_exp)
                lsum = pc if lsum is None else lsum + pc
                ps.append(pc.astype(BF16))
            pv = jnp.dot(jnp.concatenate(ps, axis=1), v, preferred_element_type=F32)
            alpha2 = jnp.concatenate([alpha] * (ATT_V_DIM // LANES), axis=1)
            if stale:
                l_prev = l_in[mi, r0:r0 + rows, :]
                acc_prev = acc_in[mi, r0:r0 + rows, :]
                m_bak[mi, r0:r0 + rows, :] = m_prev
                l_bak[mi, r0:r0 + rows, :] = l_prev
                acc_bak[mi, r0:r0 + rows, :] = acc_prev
                l_out[mi, r0:r0 + rows, :] = alpha * (l_prev + lsum)
                acc_out[mi, r0:r0 + rows, :] = alpha2 * (acc_prev + pv)
                g = pm - m_prev
                gap = g if gap is None else jnp.maximum(gap, g)
            else:
                l_out[mi, r0:r0 + rows, :] = alpha * l_in[mi, r0:r0 + rows, :] + lsum
                acc_out[mi, r0:r0 + rows, :] = alpha2 * acc_in[mi, r0:r0 + rows, :] + pv
            m_out[mi, r0:r0 + rows, :] = m_new
    if stale:
        gap_ref[...] = gap


def _attn_kernel(q_ref, k_ref, v_ref, lam_ref, nw_ref, o_ref, m_ref, l_ref, acc_ref, gap_ref,
                 *, n_kv, tq, tk, rows):
    kv = pl.program_id(3)
    live = (m_ref.at[0], l_ref.at[0], acc_ref.at[0])
    saved = (m_ref.at[1], l_ref.at[1], acc_ref.at[1])
    block = functools.partial(_attn_block, q_ref, k_ref, v_ref, gap_ref=gap_ref, tq=tq, tk=tk, rows=rows)

    @pl.when(kv == 0)
    def _init():
        m_ref[1] = jnp.full(m_ref.shape[1:], -jnp.inf, F32)
        l_ref[1] = jnp.zeros(l_ref.shape[1:], F32)
        acc_ref[1] = jnp.zeros(acc_ref.shape[1:], F32)
        gap_ref[...] = jnp.full(gap_ref.shape, 2 * STALE_MAX_LIMIT, F32)

    @pl.when(kv > 0)
    def _stale():
        block(live, saved, stale=True)

    @pl.when(jnp.max(gap_ref[...]) > STALE_MAX_LIMIT)
    def _exact():
        block(saved, live, stale=False)

    @pl.when(kv == n_kv - 1)
    def _fin():
        lam = (jnp.exp(jnp.sum(lam_ref[0:1, :] * lam_ref[1:2, :], axis=1, keepdims=True))
               - jnp.exp(jnp.sum(lam_ref[2:3, :] * lam_ref[3:4, :], axis=1, keepdims=True)) + LAMBDA_INIT)
        l0 = jnp.sum(l_ref[0, 0], axis=1, keepdims=True)
        l1 = jnp.sum(l_ref[0, 1], axis=1, keepdims=True)
        o = acc_ref[0, 0] / l0 - lam * (acc_ref[0, 1] / l1)
        o = o * lax.rsqrt(jnp.mean(o * o, axis=1, keepdims=True) + NORM_EPS)
        o_ref[...] = (o * nw_ref[...] * (1.0 - LAMBDA_INIT)).astype(o_ref.dtype)


def _attention(main, lam_par, norm_w, *, batch, seq, tq=2048, tk=1024, rows=128):
    t = main.shape[0]
    tq = min(tq, seq)
    tk = min(tk, seq)
    rows = min(rows, tq)
    nq = seq // tq
    nk = seq // tk
    vd = ATT_V_DIM
    return pl.pallas_call(
        functools.partial(_attn_kernel, n_kv=nk, tq=tq, tk=tk, rows=rows),
        grid=(batch, ATT_HEADS, nq, nk),
        in_specs=[
            pl.BlockSpec((tq, vd), lambda b, h, i, j: (b * nq + i, COL_Q // vd + h)),
            pl.BlockSpec((tk, vd), lambda b, h, i, j: (b * nk + j, COL_K // vd + h)),
            pl.BlockSpec((tk, vd), lambda b, h, i, j: (b * nk + j, COL_V // vd + h)),
            pl.BlockSpec((8, ATT_HEAD_DIM), lambda b, h, i, j: (0, 0)),
            pl.BlockSpec((1, vd), lambda b, h, i, j: (0, 0)),
        ],
        out_specs=pl.BlockSpec((tq, vd), lambda b, h, i, j: (b * nq + i, h)),
        out_shape=jax.ShapeDtypeStruct((t, ATT_HEADS * vd), BF16),
        scratch_shapes=[
            pltpu.VMEM((2, 2, tq, LANES), F32),
            pltpu.VMEM((2, 2, tq, LANES), F32),
            pltpu.VMEM((2, 2, tq, vd), F32),
            pltpu.VMEM((rows, LANES), F32),
        ],
        compiler_params=_params("parallel", "parallel", "parallel", "arbitrary"),
        name="diff_attention",
    )(main, main, main, lam_par, norm_w)


def _ssm_post_kernel(yf_ref, yb_ref, x_ref, z_ref, gs_ref, d_ref, nw_ref, w_ref, o_ref):
    y = yf_ref[...].astype(F32) + yb_ref[...].astype(F32) + d_ref[...] * x_ref[...].astype(F32)
    z = z_ref[...].astype(F32)
    y = y * (z * _sigmoid(z))
    gsz = y.shape[1] // SSM_GROUPS
    parts = []
    for g in range(SSM_GROUPS):
        seg = y[:, g * gsz:(g + 1) * gsz]
        parts.append(seg * lax.rsqrt(jnp.mean(seg * seg, axis=1, keepdims=True) + NORM_EPS))
    yn = jnp.concatenate(parts, axis=1) * nw_ref[...]
    ys = jnp.dot(yn.astype(BF16), w_ref[...], preferred_element_type=F32)
    o_ref[...] = _sigmoid(gs_ref[...].astype(F32)) * ys


def _ssm_post(y2, xbc_act, main, d_x, norm_w, w_br, *, tm=512):
    t = xbc_act.shape[0]
    tm = min(tm, t)
    d = D_MODEL
    row = lambda i: (i, 0)
    return pl.pallas_call(
        _ssm_post_kernel,
        grid=(t // tm,),
        in_specs=[
            pl.BlockSpec((None, tm, d), lambda i: (0, i, 0)),
            pl.BlockSpec((None, tm, d), lambda i: (1, i, 0)),
            pl.BlockSpec((tm, d), row),
            pl.BlockSpec((tm, d), lambda i: (i, COL_Z // d)),
            pl.BlockSpec((tm, d), lambda i: (i, COL_GS // d)),
            pl.BlockSpec((1, d), lambda i: (0, 0)),
            pl.BlockSpec((1, d), lambda i: (0, 0)),
            pl.BlockSpec((d, d), lambda i: (0, 0)),
        ],
        out_specs=pl.BlockSpec((tm, d), row),
        out_shape=jax.ShapeDtypeStruct((t, d), F32),
        compiler_params=_params("parallel"),
        name="ssm_post",
    )(y2, y2, xbc_act, main, main, d_x, norm_w, w_br)


def _mix_ln_kernel(p1_ref, oa_ref, ga_ref, h_ref, wa_ref, wo_ref, g_ref, b_ref, o_ref, ob_ref):
    ya = jnp.dot(oa_ref[...], wa_ref[...], preferred_element_type=F32)
    merged = p1_ref[...] + _sigmoid(ga_ref[...].astype(F32)) * ya
    mix = jnp.dot(merged.astype(BF16), wo_ref[...], preferred_element_type=F32)
    out = _layer_norm(ALPHA * h_ref[...] + mix, g_ref[...], b_ref[...])
    o_ref[...] = out
    ob_ref[...] = out.astype(BF16)


def _mix_ln(p1, o_att, main, h1, w_att_br, w_out, g, b, *, tm=256):
    t, d = h1.shape
    tm = min(tm, t)
    row = lambda i: (i, 0)
    const = lambda i: (0, 0)
    return pl.pallas_call(
        _mix_ln_kernel,
        grid=(t // tm,),
        in_specs=[
            pl.BlockSpec((tm, d), row),
            pl.BlockSpec((tm, d), row),
            pl.BlockSpec((tm, d), lambda i: (i, COL_GA // d)),
            pl.BlockSpec((tm, d), row),
            pl.BlockSpec((d, d), const),
            pl.BlockSpec((d, d), const),
            pl.BlockSpec((1, d), const),
            pl.BlockSpec((1, d), const),
        ],
        out_specs=[pl.BlockSpec((tm, d), row), pl.BlockSpec((tm, d), row)],
        out_shape=[jax.ShapeDtypeStruct((t, d), F32), jax.ShapeDtypeStruct((t, d), BF16)],
        compiler_params=_params("parallel"),
        name="mix_ln",
    )(p1, o_att, main, h1, w_att_br, w_out, g, b)


def _rope_tables(seq):
    half = ROT_DIM // 2
    pos = jnp.arange(seq, dtype=F32)
    inv_freq = ROPE_THETA ** (-jnp.arange(0, ROT_DIM, 2, dtype=F32) / ROT_DIM)
    ang = pos[:, None] * inv_freq[None, :]
    cos, sin = jnp.cos(ang), jnp.sin(ang)
    zeros = jnp.zeros((seq, LANES - ROT_DIM), F32)
    zh = jnp.zeros((seq, half), F32)
    cos_t = jnp.concatenate([cos, cos, jnp.ones((seq, LANES - ROT_DIM), F32)], axis=1)
    s1_t = jnp.concatenate([zh, sin, zeros], axis=1)
    s2_t = jnp.concatenate([-sin, zh, zeros], axis=1)
    return cos_t, s1_t, s2_t


def _row(v, n=None):
    v = v.astype(F32).reshape(1, -1)
    if n is not None and v.shape[1] < n:
        v = jnp.pad(v, ((0, 0), (0, n - v.shape[1])))
    return v


def _layer(x2, batch, seq, ffn1_w_gate, ffn1_w_up, ffn1_w_down, ln1_g, ln1_b, w_in, conv_w, conv_b,
           dt_bias_f, dt_bias_b, a_log_f, a_log_b, d_skip, ssm_norm_w, w_ssm_br,
           lambda_q1, lambda_k1, lambda_q2, lambda_k2, attn_norm_w, w_att_br, w_out,
           ln2_g, ln2_b, ffn2_w_gate, ffn2_w_up, ffn2_w_down, ln3_g, ln3_b):
    d = D_MODEL
    o_z, o_xbc, o_dtf, o_dtb = 0, d, d + CONV_DIM, d + CONV_DIM + SSM_HEADS
    o_q = o_dtb + SSM_HEADS
    o_k, o_v, o_gs, o_ga = o_q + d, o_q + 2 * d, o_q + 3 * d, o_q + 4 * d
    cols = lambda lo, n: w_in[:, lo:lo + n]
    w_main = jnp.concatenate(
        [cols(o_q, d), cols(o_k, d), cols(o_v, d), cols(o_z, d), cols(o_gs, d), cols(o_ga, d), cols(o_xbc, CONV_DIM)],
        axis=1).astype(BF16)
    pad = jnp.zeros((d, LANES - SSM_HEADS), w_in.dtype)
    w_dt = jnp.concatenate([cols(o_dtf, SSM_HEADS), pad, cols(o_dtb, SSM_HEADS), pad], axis=1).astype(BF16)

    h1, h1b = _ffn_ln(x2, ffn1_w_gate.astype(BF16), ffn1_w_up.astype(BF16), ffn1_w_down.astype(BF16),
                      _row(ln1_g), _row(ln1_b), emit_bf16=True)

    cos_t, s1_t, s2_t = _rope_tables(seq)
    main = _in_proj(h1b, w_main, cos_t, s1_t, s2_t, seq=seq)
    dt_raw = _dt_proj(h1b, w_dt)

    cw = jnp.pad(conv_w.astype(F32).reshape(CONV_WIDTH, CONV_DIM), ((0, 8 - CONV_WIDTH), (0, 0)))
    xbc_act = _conv_silu(main, cw, _row(conv_b), seq=seq)

    par = jnp.zeros((8, 2 * LANES), F32)
    par = par.at[0, 0:SSM_HEADS].set(dt_bias_f.astype(F32)).at[0, LANES:LANES + SSM_HEADS].set(dt_bias_b.astype(F32))
    par = par.at[1, 0:SSM_HEADS].set(a_log_f.astype(F32)).at[1, LANES:LANES + SSM_HEADS].set(a_log_b.astype(F32))
    n_x = SSM_HEADS * SSM_HEAD_DIM
    expand = (jnp.arange(LANES)[:, None] == (jnp.arange(n_x)[None, :] // SSM_HEAD_DIM)).astype(BF16)
    y2 = _ssd(xbc_act, dt_raw, par, expand, batch=batch, seq=seq)

    lam_par = jnp.zeros((8, ATT_HEAD_DIM), F32)
    for r, v in enumerate((lambda_q1, lambda_k1, lambda_q2, lambda_k2)):
        lam_par = lam_par.at[r].set(v.astype(F32))
    o_att = _attention(main, lam_par, _row(attn_norm_w), batch=batch, seq=seq)

    d_x = jnp.repeat(d_skip.astype(F32), SSM_HEAD_DIM).reshape(1, n_x)
    p1 = _ssm_post(y2, xbc_act, main, d_x, _row(ssm_norm_w), w_ssm_br.astype(BF16))
    h2, h2b = _mix_ln(p1, o_att, main, h1, w_att_br.astype(BF16), w_out.astype(BF16), _row(ln2_g), _row(ln2_b))
    del h2b
    (out,) = _ffn_ln(h2, ffn2_w_gate.astype(BF16), ffn2_w_up.astype(BF16), ffn2_w_down.astype(BF16),
                     _row(ln3_g), _row(ln3_b), emit_bf16=False)
    return out


def kernel(x, ffn1_w_gate, ffn1_w_up, ffn1_w_down, ln1_g, ln1_b, w_in, conv_w, conv_b, dt_bias_f, dt_bias_b, a_log_f, a_log_b, d_skip, ssm_norm_w, w_ssm_br, lambda_q1, lambda_k1, lambda_q2, lambda_k2, attn_norm_w, w_att_br, w_out, ln2_g, ln2_b, ffn2_w_gate, ffn2_w_up, ffn2_w_down, ln3_g, ln3_b):
    batch, seq, d = x.shape
    assert d == D_MODEL and seq % SSD_CHUNK == 0
    params = (ffn1_w_gate, ffn1_w_up, ffn1_w_down, ln1_g, ln1_b, w_in, conv_w, conv_b, dt_bias_f, dt_bias_b,
              a_log_f, a_log_b, d_skip, ssm_norm_w, w_ssm_br, lambda_q1, lambda_k1, lambda_q2, lambda_k2,
              attn_norm_w, w_att_br, w_out, ln2_g, ln2_b, ffn2_w_gate, ffn2_w_up, ffn2_w_down, ln3_g, ln3_b)
    assert all(p.shape[0] == DEPTH for p in params)
    out = _layer(x.reshape(batch * seq, d), batch, seq, *[p[0] for p in params])
    return out.reshape(batch, seq, d).astype(x.dtype)
```

```python
import functools
import math

import jax
import jax.numpy as jnp
from jax import lax
from jax.experimental import pallas as pl
from jax.experimental.pallas import tpu as pltpu

F32 = jnp.float32
BF16 = jnp.bfloat16

D_MODEL = 2048
SSM_HEAD_DIM = 64
SSM_HEADS = 32
SSM_GROUPS = 4
SSM_STATE = 128
CONV_WIDTH = 5
CONV_DIM = 3072
SSD_CHUNK = 128
ATT_HEADS = 8
ATT_HEAD_DIM = 128
ATT_V_DIM = 256
ROPE_THETA = 500000.0
ROT_DIM = 32
D_FF = 5632
DEPTH = 1
ALPHA = (2 * DEPTH) ** 0.25
NORM_EPS = 1e-5
LAMBDA_INIT = 0.8 - 0.6 * math.exp(-0.3 * 0)

LANES = 128
VMEM_LIMIT = 56 * 2 ** 20

QK_Q, QK_K = 0, 2048
VZG_V, VZG_Z, VZG_GS, VZG_GA = 0, 2048, 4096, 6144
Q_SCALE = ATT_HEAD_DIM ** -0.5 * math.log2(math.e)


def _params(*sem):
    return pltpu.CompilerParams(dimension_semantics=sem, vmem_limit_bytes=VMEM_LIMIT)


def _layer_norm(y, g, b):
    mu = jnp.mean(y, axis=-1, keepdims=True)
    yc = y - mu
    var = jnp.mean(yc * yc, axis=-1, keepdims=True)
    return yc * lax.rsqrt(var + NORM_EPS) * g + b


def _sigmoid(v):
    return 1.0 / (1.0 + jnp.exp(-v))


def _ffn_ln_kernel(x_ref, wg_ref, wu_ref, wd_ref, g_ref, b_ref, *rest, n_f, emit_bf16):
    if emit_bf16:
        o_ref, ob_ref, xb_ref, acc_ref = rest
    else:
        o_ref, xb_ref, acc_ref = rest
    f = pl.program_id(1)

    @pl.when(f == 0)
    def _init():
        xb_ref[...] = x_ref[...].astype(BF16)
        acc_ref[...] = jnp.zeros_like(acc_ref)

    xb = xb_ref[...]
    hg = jnp.dot(xb, wg_ref[...], preferred_element_type=F32)
    hu = jnp.dot(xb, wu_ref[...], preferred_element_type=F32)
    h = hg * _sigmoid(hg) * hu
    acc_ref[...] += jnp.dot(h.astype(BF16), wd_ref[...], preferred_element_type=F32)

    @pl.when(f == n_f - 1)
    def _fin():
        y = ALPHA * x_ref[...] + 0.5 * acc_ref[...]
        out = _layer_norm(y, g_ref[...], b_ref[...])
        o_ref[...] = out
        if emit_bf16:
            ob_ref[...] = out.astype(BF16)


def _ffn_ln(x, wg, wu, wd, g, b, *, emit_bf16, tm=512, tf=512):
    t, d = x.shape
    dff = wg.shape[1]
    tm = min(tm, t)
    n_f = dff // tf
    out_shape = [jax.ShapeDtypeStruct((t, d), F32)]
    out_specs = [pl.BlockSpec((tm, d), lambda i, f: (i, 0))]
    if emit_bf16:
        out_shape.append(jax.ShapeDtypeStruct((t, d), BF16))
        out_specs.append(pl.BlockSpec((tm, d), lambda i, f: (i, 0)))
    return pl.pallas_call(
        functools.partial(_ffn_ln_kernel, n_f=n_f, emit_bf16=emit_bf16),
        grid=(t // tm, n_f),
        in_specs=[
            pl.BlockSpec((tm, d), lambda i, f: (i, 0)),
            pl.BlockSpec((d, tf), lambda i, f: (0, f)),
            pl.BlockSpec((d, tf), lambda i, f: (0, f)),
            pl.BlockSpec((tf, d), lambda i, f: (f, 0)),
            pl.BlockSpec((1, d), lambda i, f: (0, 0)),
            pl.BlockSpec((1, d), lambda i, f: (0, 0)),
        ],
        out_specs=out_specs,
        out_shape=out_shape,
        scratch_shapes=[pltpu.VMEM((tm, d), BF16), pltpu.VMEM((tm, d), F32)],
        compiler_params=_params("parallel", "arbitrary"),
        name="ffn_ln",
    )(x, wg, wu, wd, g, b)


def _qk_proj_kernel(x_ref, w_ref, cos_ref, s1_ref, s2_ref, o_ref, *, n_q, tn):
    j = pl.program_id(1)
    y = jnp.dot(x_ref[...], w_ref[...], preferred_element_type=F32)
    sc = jnp.where(j < n_q, Q_SCALE, 1.0).astype(F32)
    c = cos_ref[...] * sc
    s1 = s1_ref[...] * sc
    s2 = s2_ref[...] * sc
    half = ROT_DIM // 2
    for k in range(tn // LANES):
        yk = y[:, k * LANES:(k + 1) * LANES]
        r = yk * c + pltpu.roll(yk, half, 1) * s1 + pltpu.roll(yk, LANES - half, 1) * s2
        o_ref[:, k * LANES:(k + 1) * LANES] = r.astype(o_ref.dtype)


def _qk_proj(hb, w_qk, cos_t, s1_t, s2_t, *, seq, tm=1024, tn=1024):
    t, d = hb.shape
    n = w_qk.shape[1]
    tm = min(tm, seq)
    tiles_per_seq = seq // tm
    tab = pl.BlockSpec((tm, LANES), lambda i, j: (i % tiles_per_seq, 0))
    return pl.pallas_call(
        functools.partial(_qk_proj_kernel, n_q=QK_K // tn, tn=tn),
        grid=(t // tm, n // tn),
        in_specs=[
            pl.BlockSpec((tm, d), lambda i, j: (i, 0)),
            pl.BlockSpec((d, tn), lambda i, j: (0, j)),
            tab, tab, tab,
        ],
        out_specs=pl.BlockSpec((tm, tn), lambda i, j: (i, j)),
        out_shape=jax.ShapeDtypeStruct((t, n), BF16),
        compiler_params=_params("parallel", "arbitrary"),
        name="qk_proj",
    )(hb, w_qk, cos_t, s1_t, s2_t)


def _proj_kernel(x_ref, w_ref, o_ref):
    o_ref[...] = jnp.dot(x_ref[...], w_ref[...], preferred_element_type=F32).astype(o_ref.dtype)


def _proj(hb, w, out_dtype, *, name, tm=1024, tn=1024):
    t, d = hb.shape
    n = w.shape[1]
    tm = min(tm, t)
    tn = min(tn, n)
    return pl.pallas_call(
        _proj_kernel,
        grid=(t // tm, n // tn),
        in_specs=[pl.BlockSpec((tm, d), lambda i, j: (i, 0)), pl.BlockSpec((d, tn), lambda i, j: (0, j))],
        out_specs=pl.BlockSpec((tm, tn), lambda i, j: (i, j)),
        out_shape=jax.ShapeDtypeStruct((t, n), out_dtype),
        compiler_params=_params("parallel", "arbitrary"),
        name=name,
    )(hb, w)


HALO = 16


def _conv_kernel(cur_ref, prev_ref, next_ref, w_ref, b_ref, o_ref, ext_ref, *, tm, tiles_per_seq):
    i = pl.program_id(0)
    first = (i % tiles_per_seq) == 0
    last = (i % tiles_per_seq) == tiles_per_seq - 1
    ext_ref[0:HALO, :] = jnp.where(first, 0.0, prev_ref[...].astype(F32))
    ext_ref[HALO:HALO + tm, :] = cur_ref[...].astype(F32)
    ext_ref[HALO + tm:2 * HALO + tm, :] = jnp.where(last, 0.0, next_ref[...].astype(F32))
    pad = CONV_WIDTH // 2
    acc = b_ref[...] + ext_ref[HALO - pad:HALO - pad + tm, :] * w_ref[0:1, :]
    for w in range(1, CONV_WIDTH):
        acc = acc + ext_ref[HALO - pad + w:HALO - pad + w + tm, :] * w_ref[w:w + 1, :]
    o_ref[...] = (acc * _sigmoid(acc)).astype(o_ref.dtype)


def _conv_silu(main, conv_w, conv_b, *, seq, tm=256):
    t = main.shape[0]
    tm = min(tm, seq)
    tiles_per_seq = seq // tm
    cb = 0
    r = tm // HALO
    nblk = t // HALO
    return pl.pallas_call(
        functools.partial(_conv_kernel, tm=tm, tiles_per_seq=tiles_per_seq),
        grid=(t // tm,),
        in_specs=[
            pl.BlockSpec((tm, CONV_DIM), lambda i: (i, cb)),
            pl.BlockSpec((HALO, CONV_DIM), lambda i: (jnp.maximum(i * r - 1, 0), cb)),
            pl.BlockSpec((HALO, CONV_DIM), lambda i: (jnp.minimum((i + 1) * r, nblk - 1), cb)),
            pl.BlockSpec((8, CONV_DIM), lambda i: (0, 0)),
            pl.BlockSpec((1, CONV_DIM), lambda i: (0, 0)),
        ],
        out_specs=pl.BlockSpec((tm, CONV_DIM), lambda i: (i, 0)),
        out_shape=jax.ShapeDtypeStruct((t, CONV_DIM), BF16),
        scratch_shapes=[pltpu.VMEM((tm + 2 * HALO, CONV_DIM), F32)],
        compiler_params=_params("parallel"),
        name="conv_silu",
    )(main, main, main, conv_w, conv_b)


def _split_dot(v, m01, terms):
    out = None
    r = v
    for _ in range(terms):
        part = r.astype(BF16)
        d = jnp.dot(part, m01, preferred_element_type=F32)
        out = d if out is None else out + d
        r = r - part.astype(F32)
    return out


def _ssd_kernel(xbc_ref, dt_ref, par_ref, e_ref, y_ref, h_ref):
    L = SSD_CHUNK
    d = pl.program_id(1)
    c = pl.program_id(2)

    @pl.when(c == 0)
    def _init():
        h_ref[...] = jnp.zeros_like(h_ref)

    fwd = d == 0
    pre = dt_ref[...] + par_ref[0:1, :]
    dt = jnp.maximum(pre, 0.0) + jnp.log(1.0 + jnp.exp(-jnp.abs(pre)))
    a = dt * (-jnp.exp(par_ref[1:2, :]))

    row = lax.broadcasted_iota(jnp.int32, (L, L), 0)
    col = lax.broadcasted_iota(jnp.int32, (L, L), 1)
    msk = (col - row) * (1 - 2 * d) <= 0
    tri = jnp.where(msk, 1.0, 0.0).astype(BF16)
    c_col = _tri_cumsum(tri, a)
    c_row = c_col.T
    dt_row = dt.T
    tot = jnp.where(fwd, c_col[L - 1:L, :], c_col[0:1, :])
    e_out = jnp.exp(c_col)
    f_in = dt * jnp.exp(tot - c_col)
    cd = jnp.broadcast_to(jnp.exp(tot), (8, LANES))
    stacked = jnp.concatenate([f_in, e_out, cd], axis=0)
    ex = _split_dot(stacked, e_ref[...], 2)
    f_x = ex[0:L]
    e_x = ex[L:2 * L]
    cd_x = ex[2 * L:2 * L + 1]

    n_x = SSM_HEADS * SSM_HEAD_DIM
    gn = SSM_GROUPS * SSM_STATE
    xw = (xbc_ref[:, 0:n_x].astype(F32) * f_x).astype(BF16)
    lane = lax.broadcasted_iota(jnp.int32, (L, LANES), 1)
    gw = n_x // SSM_GROUPS
    for g in range(SSM_GROUPS):
        bg = xbc_ref[:, n_x + g * SSM_STATE:n_x + (g + 1) * SSM_STATE]
        cg = xbc_ref[:, n_x + gn + g * SSM_STATE:n_x + gn + (g + 1) * SSM_STATE]
        bgt = bg.astype(F32).T.astype(BF16)
        gmat = jnp.dot(cg, bgt, preferred_element_type=F32)
        hg = h_ref[:, g * gw:(g + 1) * gw]
        yoff = jnp.dot(cg, hg.astype(BF16), preferred_element_type=F32) * e_x[:, g * gw:(g + 1) * gw]
        st = jnp.dot(bgt, xw[:, g * gw:(g + 1) * gw], preferred_element_type=F32)
        h_ref[:, g * gw:(g + 1) * gw] = cd_x[:, g * gw:(g + 1) * gw] * hg + st
        for jp in range(gw // LANES):
            lo = g * gw + jp * LANES
            xp = xbc_ref[:, lo:lo + LANES]
            outs = []
            for hh in range(2):
                h = lo // SSM_HEAD_DIM + hh
                seg = c_col[:, h:h + 1] - c_row[h:h + 1, :]
                decay = jnp.exp(jnp.where(msk, seg, -jnp.inf))
                m = (gmat * (decay * dt_row[h:h + 1, :])).astype(BF16)
                outs.append(jnp.dot(m, xp, preferred_element_type=F32))
            ydiag = jnp.where(lane < SSM_HEAD_DIM, outs[0], outs[1])
            y_ref[:, lo:lo + LANES] = (ydiag + yoff[:, jp * LANES:(jp + 1) * LANES]).astype(y_ref.dtype)


def _tri_cumsum(tri, a):
    out = None
    r = a
    for _ in range(3):
        part = r.astype(BF16)
        d = jnp.dot(tri, part, preferred_element_type=F32)
        out = d if out is None else out + d
        r = r - part.astype(F32)
    return out


def _ssd(xbc_act, dt_raw, par, expand, *, batch, seq):
    t = xbc_act.shape[0]
    L = SSD_CHUNK
    nc = seq // L
    n_x = SSM_HEADS * SSM_HEAD_DIM

    def chunk(b, d, c):
        return b * nc + c + d * (nc - 1 - 2 * c)

    return pl.pallas_call(
        _ssd_kernel,
        grid=(batch, 2, nc),
        in_specs=[
            pl.BlockSpec((L, CONV_DIM), lambda b, d, c: (chunk(b, d, c), 0)),
            pl.BlockSpec((L, LANES), lambda b, d, c: (chunk(b, d, c), d)),
            pl.BlockSpec((8, LANES), lambda b, d, c: (0, d)),
            pl.BlockSpec((LANES, n_x), lambda b, d, c: (0, 0)),
        ],
        out_specs=pl.BlockSpec((None, L, n_x), lambda b, d, c: (d, chunk(b, d, c), 0)),
        out_shape=jax.ShapeDtypeStruct((2, t, n_x), BF16),
        scratch_shapes=[pltpu.VMEM((SSM_STATE, n_x), F32)],
        compiler_params=_params("parallel", "parallel", "arbitrary"),
        name="ssd",
    )(xbc_act, dt_raw, par, expand)


STALE_MAX_LIMIT = 64.0


def _attn_block(q_ref, k_ref, v_ref, st_in, st_out, gap_ref, *, stale, tq, tk, rows):
    m_in, l_in, acc_in = st_in
    m_out, l_out, acc_out = st_out
    if stale:
        m_bak, l_bak, acc_bak = st_out
        m_out, l_out, acc_out = st_in
    v = v_ref[...]
    n_lt = tk // LANES
    gap = None
    for rb in range(tq // rows):
        r0 = rb * rows
        for mi in range(2):
            q = q_ref[r0:r0 + rows, mi * ATT_HEAD_DIM:(mi + 1) * ATT_HEAD_DIM]
            k = k_ref[:, mi * ATT_HEAD_DIM:(mi + 1) * ATT_HEAD_DIM]
            s = lax.dot_general(q, k, (((1,), (1,)), ((), ())), preferred_element_type=F32)
            m_prev = m_in[mi, r0:r0 + rows, :]
            pm = s[:, 0:LANES]
            for c in range(1, n_lt):
                pm = jnp.maximum(pm, s[:, c * LANES:(c + 1) * LANES])
            m_new = jnp.maximum(m_prev, jnp.max(pm, axis=1, keepdims=True))
            alpha = jnp.exp2(m_prev - m_new)
            m_exp = m_prev if stale else m_new
            lsum = None
            ps = []
            for c in range(n_lt):
                pc = jnp.exp2(s[:, c * LANES:(c + 1) * LANES] - m_exp)
                lsum = pc if lsum is None else lsum + pc
                ps.append(pc.astype(BF16))
            pv = jnp.dot(jnp.concatenate(ps, axis=1), v, preferred_element_type=F32)
            alpha2 = jnp.concatenate([alpha] * (ATT_V_DIM // LANES), axis=1)
            if stale:
                l_prev = l_in[mi, r0:r0 + rows, :]
                acc_prev = acc_in[mi, r0:r0 + rows, :]
                m_bak[mi, r0:r0 + rows, :] = m_prev
                l_bak[mi, r0:r0 + rows, :] = l_prev
                acc_bak[mi, r0:r0 + rows, :] = acc_prev
                l_out[mi, r0:r0 + rows, :] = alpha * (l_prev + lsum)
                acc_out[mi, r0:r0 + rows, :] = alpha2 * (acc_prev + pv)
                g = pm - m_prev
                gap = g if gap is None else jnp.maximum(gap, g)
            else:
                l_out[mi, r0:r0 + rows, :] = alpha * l_in[mi, r0:r0 + rows, :] + lsum
                acc_out[mi, r0:r0 + rows, :] = alpha2 * acc_in[mi, r0:r0 + rows, :] + pv
            m_out[mi, r0:r0 + rows, :] = m_new
    if stale:
        gap_ref[...] = gap


def _attn_kernel(q_ref, k_ref, v_ref, lam_ref, nw_ref, o_ref, m_ref, l_ref, acc_ref, gap_ref,
                 *, n_kv, tq, tk, rows):
    kv = pl.program_id(3)
    live = (m_ref.at[0], l_ref.at[0], acc_ref.at[0])
    saved = (m_ref.at[1], l_ref.at[1], acc_ref.at[1])
    block = functools.partial(_attn_block, q_ref, k_ref, v_ref, gap_ref=gap_ref, tq=tq, tk=tk, rows=rows)

    @pl.when(kv == 0)
    def _init():
        m_ref[1] = jnp.full(m_ref.shape[1:], -jnp.inf, F32)
        l_ref[1] = jnp.zeros(l_ref.shape[1:], F32)
        acc_ref[1] = jnp.zeros(acc_ref.shape[1:], F32)
        gap_ref[...] = jnp.full(gap_ref.shape, 2 * STALE_MAX_LIMIT, F32)

    @pl.when(kv > 0)
    def _stale():
        block(live, saved, stale=True)

    @pl.when(jnp.max(gap_ref[...]) > STALE_MAX_LIMIT)
    def _exact():
        block(saved, live, stale=False)

    @pl.when(kv == n_kv - 1)
    def _fin():
        lam = (jnp.exp(jnp.sum(lam_ref[0:1, :] * lam_ref[1:2, :], axis=1, keepdims=True))
               - jnp.exp(jnp.sum(lam_ref[2:3, :] * lam_ref[3:4, :], axis=1, keepdims=True)) + LAMBDA_INIT)
        l0 = jnp.sum(l_ref[0, 0], axis=1, keepdims=True)
        l1 = jnp.sum(l_ref[0, 1], axis=1, keepdims=True)
        o = acc_ref[0, 0] / l0 - lam * (acc_ref[0, 1] / l1)
        o = o * lax.rsqrt(jnp.mean(o * o, axis=1, keepdims=True) + NORM_EPS)
        o_ref[...] = (o * nw_ref[...] * (1.0 - LAMBDA_INIT)).astype(o_ref.dtype)


def _attention(qk, vzg, lam_par, norm_w, *, batch, seq, tq=2048, tk=1024, rows=128):
    t = qk.shape[0]
    tq = min(tq, seq)
    tk = min(tk, seq)
    rows = min(rows, tq)
    nq = seq // tq
    nk = seq // tk
    vd = ATT_V_DIM
    return pl.pallas_call(
        functools.partial(_attn_kernel, n_kv=nk, tq=tq, tk=tk, rows=rows),
        grid=(batch, ATT_HEADS, nq, nk),
        in_specs=[
            pl.BlockSpec((tq, vd), lambda b, h, i, j: (b * nq + i, QK_Q // vd + h)),
            pl.BlockSpec((tk, vd), lambda b, h, i, j: (b * nk + j, QK_K // vd + h)),
            pl.BlockSpec((tk, vd), lambda b, h, i, j: (b * nk + j, VZG_V // vd + h)),
            pl.BlockSpec((8, ATT_HEAD_DIM), lambda b, h, i, j: (0, 0)),
            pl.BlockSpec((1, vd), lambda b, h, i, j: (0, 0)),
        ],
        out_specs=pl.BlockSpec((tq, vd), lambda b, h, i, j: (b * nq + i, h)),
        out_shape=jax.ShapeDtypeStruct((t, ATT_HEADS * vd), BF16),
        scratch_shapes=[
            pltpu.VMEM((2, 2, tq, LANES), F32),
            pltpu.VMEM((2, 2, tq, LANES), F32),
            pltpu.VMEM((2, 2, tq, vd), F32),
            pltpu.VMEM((rows, LANES), F32),
        ],
        compiler_params=_params("parallel", "parallel", "parallel", "arbitrary"),
        name="diff_attention",
    )(qk, qk, vzg, lam_par, norm_w)


def _ssm_post_kernel(yf_ref, yb_ref, x_ref, z_ref, gs_ref, d_ref, nw_ref, w_ref, o_ref):
    y = yf_ref[...].astype(F32) + yb_ref[...].astype(F32) + d_ref[...] * x_ref[...].astype(F32)
    z = z_ref[...].astype(F32)
    y = y * (z * _sigmoid(z))
    gsz = y.shape[1] // SSM_GROUPS
    parts = []
    for g in range(SSM_GROUPS):
        seg = y[:, g * gsz:(g + 1) * gsz]
        parts.append(seg * lax.rsqrt(jnp.mean(seg * seg, axis=1, keepdims=True) + NORM_EPS))
    yn = jnp.concatenate(parts, axis=1) * nw_ref[...]
    ys = jnp.dot(yn.astype(BF16), w_ref[...], preferred_element_type=F32)
    o_ref[...] = _sigmoid(gs_ref[...].astype(F32)) * ys


def _ssm_post(y2, xbc_act, main, d_x, norm_w, w_br, *, tm=512):
    t = xbc_act.shape[0]
    tm = min(tm, t)
    d = D_MODEL
    row = lambda i: (i, 0)
    return pl.pallas_call(
        _ssm_post_kernel,
        grid=(t // tm,),
        in_specs=[
            pl.BlockSpec((None, tm, d), lambda i: (0, i, 0)),
            pl.BlockSpec((None, tm, d), lambda i: (1, i, 0)),
            pl.BlockSpec((tm, d), row),
            pl.BlockSpec((tm, d), lambda i: (i, VZG_Z // d)),
            pl.BlockSpec((tm, d), lambda i: (i, VZG_GS // d)),
            pl.BlockSpec((1, d), lambda i: (0, 0)),
            pl.BlockSpec((1, d), lambda i: (0, 0)),
            pl.BlockSpec((d, d), lambda i: (0, 0)),
        ],
        out_specs=pl.BlockSpec((tm, d), row),
        out_shape=jax.ShapeDtypeStruct((t, d), F32),
        compiler_params=_params("parallel"),
        name="ssm_post",
    )(y2, y2, xbc_act, main, main, d_x, norm_w, w_br)


def _mix_ln_kernel(p1_ref, oa_ref, ga_ref, h_ref, wa_ref, wo_ref, g_ref, b_ref, o_ref, ob_ref):
    ya = jnp.dot(oa_ref[...], wa_ref[...], preferred_element_type=F32)
    merged = p1_ref[...] + _sigmoid(ga_ref[...].astype(F32)) * ya
    mix = jnp.dot(merged.astype(BF16), wo_ref[...], preferred_element_type=F32)
    out = _layer_norm(ALPHA * h_ref[...] + mix, g_ref[...], b_ref[...])
    o_ref[...] = out
    ob_ref[...] = out.astype(BF16)


def _mix_ln(p1, o_att, main, h1, w_att_br, w_out, g, b, *, tm=256):
    t, d = h1.shape
    tm = min(tm, t)
    row = lambda i: (i, 0)
    const = lambda i: (0, 0)
    return pl.pallas_call(
        _mix_ln_kernel,
        grid=(t // tm,),
        in_specs=[
            pl.BlockSpec((tm, d), row),
            pl.BlockSpec((tm, d), row),
            pl.BlockSpec((tm, d), lambda i: (i, VZG_GA // d)),
            pl.BlockSpec((tm, d), row),
            pl.BlockSpec((d, d), const),
            pl.BlockSpec((d, d), const),
            pl.BlockSpec((1, d), const),
            pl.BlockSpec((1, d), const),
        ],
        out_specs=[pl.BlockSpec((tm, d), row), pl.BlockSpec((tm, d), row)],
        out_shape=[jax.ShapeDtypeStruct((t, d), F32), jax.ShapeDtypeStruct((t, d), BF16)],
        compiler_params=_params("parallel"),
        name="mix_ln",
    )(p1, o_att, main, h1, w_att_br, w_out, g, b)


def _rope_tables(seq):
    half = ROT_DIM // 2
    pos = jnp.arange(seq, dtype=F32)
    inv_freq = ROPE_THETA ** (-jnp.arange(0, ROT_DIM, 2, dtype=F32) / ROT_DIM)
    ang = pos[:, None] * inv_freq[None, :]
    cos, sin = jnp.cos(ang), jnp.sin(ang)
    zeros = jnp.zeros((seq, LANES - ROT_DIM), F32)
    zh = jnp.zeros((seq, half), F32)
    cos_t = jnp.concatenate([cos, cos, jnp.ones((seq, LANES - ROT_DIM), F32)], axis=1)
    s1_t = jnp.concatenate([zh, sin, zeros], axis=1)
    s2_t = jnp.concatenate([-sin, zh, zeros], axis=1)
    return cos_t, s1_t, s2_t


def _row(v, n=None):
    v = v.astype(F32).reshape(1, -1)
    if n is not None and v.shape[1] < n:
        v = jnp.pad(v, ((0, 0), (0, n - v.shape[1])))
    return v


def _layer(x2, batch, seq, ffn1_w_gate, ffn1_w_up, ffn1_w_down, ln1_g, ln1_b, w_in, conv_w, conv_b,
           dt_bias_f, dt_bias_b, a_log_f, a_log_b, d_skip, ssm_norm_w, w_ssm_br,
           lambda_q1, lambda_k1, lambda_q2, lambda_k2, attn_norm_w, w_att_br, w_out,
           ln2_g, ln2_b, ffn2_w_gate, ffn2_w_up, ffn2_w_down, ln3_g, ln3_b):
    d = D_MODEL
    o_z, o_xbc, o_dtf, o_dtb = 0, d, d + CONV_DIM, d + CONV_DIM + SSM_HEADS
    o_q = o_dtb + SSM_HEADS
    o_k, o_v, o_gs, o_ga = o_q + d, o_q + 2 * d, o_q + 3 * d, o_q + 4 * d
    cols = lambda lo, n: w_in[:, lo:lo + n]
    w_qk = cols(o_q, 2 * d).astype(BF16)
    w_vzg = jnp.concatenate([cols(o_v, d), cols(o_z, d), cols(o_gs, 2 * d)], axis=1).astype(BF16)
    w_xbc = cols(o_xbc, CONV_DIM).astype(BF16)
    pad = jnp.zeros((d, LANES - SSM_HEADS), w_in.dtype)
    w_dt = jnp.concatenate([cols(o_dtf, SSM_HEADS), pad, cols(o_dtb, SSM_HEADS), pad], axis=1).astype(BF16)

    h1, h1b = _ffn_ln(x2, ffn1_w_gate.astype(BF16), ffn1_w_up.astype(BF16), ffn1_w_down.astype(BF16),
                      _row(ln1_g), _row(ln1_b), emit_bf16=True)

    cos_t, s1_t, s2_t = _rope_tables(seq)
    qk = _qk_proj(h1b, w_qk, cos_t, s1_t, s2_t, seq=seq)
    vzg = _proj(h1b, w_vzg, BF16, name="vzg_proj")
    xbc = _proj(h1b, w_xbc, BF16, name="xbc_proj")
    dt_raw = _proj(h1b, w_dt, F32, name="dt_proj")

    cw = jnp.pad(conv_w.astype(F32).reshape(CONV_WIDTH, CONV_DIM), ((0, 8 - CONV_WIDTH), (0, 0)))
    xbc_act = _conv_silu(xbc, cw, _row(conv_b), seq=seq)

    par = jnp.zeros((8, 2 * LANES), F32)
    par = par.at[0, 0:SSM_HEADS].set(dt_bias_f.astype(F32)).at[0, LANES:LANES + SSM_HEADS].set(dt_bias_b.astype(F32))
    par = par.at[1, 0:SSM_HEADS].set(a_log_f.astype(F32)).at[1, LANES:LANES + SSM_HEADS].set(a_log_b.astype(F32))
    n_x = SSM_HEADS * SSM_HEAD_DIM
    expand = (jnp.arange(LANES)[:, None] == (jnp.arange(n_x)[None, :] // SSM_HEAD_DIM)).astype(BF16)
    y2 = _ssd(xbc_act, dt_raw, par, expand, batch=batch, seq=seq)

    lam_par = jnp.zeros((8, ATT_HEAD_DIM), F32)
    for r, v in enumerate((lambda_q1, lambda_k1, lambda_q2, lambda_k2)):
        lam_par = lam_par.at[r].set(v.astype(F32))
    o_att = _attention(qk, vzg, lam_par, _row(attn_norm_w), batch=batch, seq=seq)

    d_x = jnp.repeat(d_skip.astype(F32), SSM_HEAD_DIM).reshape(1, n_x)
    p1 = _ssm_post(y2, xbc_act, vzg, d_x, _row(ssm_norm_w), w_ssm_br.astype(BF16))
    h2, h2b = _mix_ln(p1, o_att, vzg, h1, w_att_br.astype(BF16), w_out.astype(BF16), _row(ln2_g), _row(ln2_b))
    del h2b
    (out,) = _ffn_ln(h2, ffn2_w_gate.astype(BF16), ffn2_w_up.astype(BF16), ffn2_w_down.astype(BF16),
                     _row(ln3_g), _row(ln3_b), emit_bf16=False)
    return out


def kernel(x, ffn1_w_gate, ffn1_w_up, ffn1_w_down, ln1_g, ln1_b, w_in, conv_w, conv_b, dt_bias_f, dt_bias_b, a_log_f, a_log_b, d_skip, ssm_norm_w, w_ssm_br, lambda_q1, lambda_k1, lambda_q2, lambda_k2, attn_norm_w, w_att_br, w_out, ln2_g, ln2_b, ffn2_w_gate, ffn2_w_up, ffn2_w_down, ln3_g, ln3_b):
    batch, seq, d = x.shape
    assert d == D_MODEL and seq % SSD_CHUNK == 0
    params = (ffn1_w_gate, ffn1_w_up, ffn1_w_down, ln1_g, ln1_b, w_in, conv_w, conv_b, dt_bias_f, dt_bias_b,
              a_log_f, a_log_b, d_skip, ssm_norm_w, w_ssm_br, lambda_q1, lambda_k1, lambda_q2, lambda_k2,
              attn_norm_w, w_att_br, w_out, ln2_g, ln2_b, ffn2_w_gate, ffn2_w_up, ffn2_w_down, ln3_g, ln3_b)
    assert all(p.shape[0] == DEPTH for p in params)
    out = _layer(x.reshape(batch * seq, d), batch, seq, *[p[0] for p in params])
    return out.reshape(batch, seq, d).astype(x.dtype)
```

```python
import functools
import math

import jax
import jax.numpy as jnp
from jax import lax
from jax.experimental import pallas as pl
from jax.experimental.pallas import tpu as pltpu

F32 = jnp.float32
BF16 = jnp.bfloat16

D_MODEL = 2048
SSM_HEAD_DIM = 64
SSM_HEADS = 32
SSM_GROUPS = 4
SSM_STATE = 128
CONV_WIDTH = 5
CONV_DIM = 3072
SSD_CHUNK = 128
ATT_HEADS = 8
ATT_HEAD_DIM = 128
ATT_V_DIM = 256
ROPE_THETA = 500000.0
ROT_DIM = 32
D_FF = 5632
DEPTH = 1
ALPHA = (2 * DEPTH) ** 0.25
NORM_EPS = 1e-5
LAMBDA_INIT = 0.8 - 0.6 * math.exp(-0.3 * 0)

LANES = 128
VMEM_LIMIT = 56 * 2 ** 20

QK_Q, QK_K = 0, 2048
VZG_V, VZG_Z, VZG_GS, VZG_GA = 0, 2048, 4096, 6144
Q_SCALE = ATT_HEAD_DIM ** -0.5 * math.log2(math.e)


def _params(*sem):
    return pltpu.CompilerParams(dimension_semantics=sem, vmem_limit_bytes=VMEM_LIMIT)


def _layer_norm(y, g, b):
    mu = jnp.mean(y, axis=-1, keepdims=True)
    yc = y - mu
    var = jnp.mean(yc * yc, axis=-1, keepdims=True)
    return yc * lax.rsqrt(var + NORM_EPS) * g + b


def _sigmoid(v):
    return 1.0 / (1.0 + jnp.exp(-v))


def _ffn_ln_kernel(x_ref, wg_ref, wu_ref, wd_ref, g_ref, b_ref, *rest, n_f, emit_bf16):
    if emit_bf16:
        o_ref, ob_ref, xb_ref, acc_ref = rest
    else:
        o_ref, xb_ref, acc_ref = rest
    f = pl.program_id(1)

    @pl.when(f == 0)
    def _init():
        xb_ref[...] = x_ref[...].astype(BF16)
        acc_ref[...] = jnp.zeros_like(acc_ref)

    xb = xb_ref[...]
    hg = jnp.dot(xb, wg_ref[...], preferred_element_type=F32)
    hu = jnp.dot(xb, wu_ref[...], preferred_element_type=F32)
    h = hg * _sigmoid(hg) * hu
    acc_ref[...] += jnp.dot(h.astype(BF16), wd_ref[...], preferred_element_type=F32)

    @pl.when(f == n_f - 1)
    def _fin():
        y = ALPHA * x_ref[...] + 0.5 * acc_ref[...]
        out = _layer_norm(y, g_ref[...], b_ref[...])
        o_ref[...] = out
        if emit_bf16:
            ob_ref[...] = out.astype(BF16)


def _ffn_ln(x, wg, wu, wd, g, b, *, emit_bf16, tm=512, tf=512):
    t, d = x.shape
    dff = wg.shape[1]
    tm = min(tm, t)
    n_f = dff // tf
    out_shape = [jax.ShapeDtypeStruct((t, d), F32)]
    out_specs = [pl.BlockSpec((tm, d), lambda i, f: (i, 0))]
    if emit_bf16:
        out_shape.append(jax.ShapeDtypeStruct((t, d), BF16))
        out_specs.append(pl.BlockSpec((tm, d), lambda i, f: (i, 0)))
    return pl.pallas_call(
        functools.partial(_ffn_ln_kernel, n_f=n_f, emit_bf16=emit_bf16),
        grid=(t // tm, n_f),
        in_specs=[
            pl.BlockSpec((tm, d), lambda i, f: (i, 0)),
            pl.BlockSpec((d, tf), lambda i, f: (0, f)),
            pl.BlockSpec((d, tf), lambda i, f: (0, f)),
            pl.BlockSpec((tf, d), lambda i, f: (f, 0)),
            pl.BlockSpec((1, d), lambda i, f: (0, 0)),
            pl.BlockSpec((1, d), lambda i, f: (0, 0)),
        ],
        out_specs=out_specs,
        out_shape=out_shape,
        scratch_shapes=[pltpu.VMEM((tm, d), BF16), pltpu.VMEM((tm, d), F32)],
        compiler_params=_params("parallel", "arbitrary"),
        name="ffn_ln",
    )(x, wg, wu, wd, g, b)


def _qk_proj_kernel(x_ref, w_ref, cos_ref, s1_ref, s2_ref, o_ref, *, n_q, tn):
    j = pl.program_id(1)
    y = jnp.dot(x_ref[...], w_ref[...], preferred_element_type=F32)
    sc = jnp.where(j < n_q, Q_SCALE, 1.0).astype(F32)
    c = cos_ref[...] * sc
    s1 = s1_ref[...] * sc
    s2 = s2_ref[...] * sc
    half = ROT_DIM // 2
    for k in range(tn // LANES):
        yk = y[:, k * LANES:(k + 1) * LANES]
        r = yk * c + pltpu.roll(yk, half, 1) * s1 + pltpu.roll(yk, LANES - half, 1) * s2
        o_ref[:, k * LANES:(k + 1) * LANES] = r.astype(o_ref.dtype)


def _qk_proj(hb, w_qk, cos_t, s1_t, s2_t, *, seq, tm=1024, tn=1024):
    t, d = hb.shape
    n = w_qk.shape[1]
    tm = min(tm, seq)
    tiles_per_seq = seq // tm
    tab = pl.BlockSpec((tm, LANES), lambda i, j: (i % tiles_per_seq, 0))
    return pl.pallas_call(
        functools.partial(_qk_proj_kernel, n_q=QK_K // tn, tn=tn),
        grid=(t // tm, n // tn),
        in_specs=[
            pl.BlockSpec((tm, d), lambda i, j: (i, 0)),
            pl.BlockSpec((d, tn), lambda i, j: (0, j)),
            tab, tab, tab,
        ],
        out_specs=pl.BlockSpec((tm, tn), lambda i, j: (i, j)),
        out_shape=jax.ShapeDtypeStruct((t, n), BF16),
        compiler_params=_params("parallel", "arbitrary"),
        name="qk_proj",
    )(hb, w_qk, cos_t, s1_t, s2_t)


def _proj_kernel(x_ref, w_ref, o_ref):
    o_ref[...] = jnp.dot(x_ref[...], w_ref[...], preferred_element_type=F32).astype(o_ref.dtype)


def _proj(hb, w, out_dtype, *, name, tm=1024, tn=1024):
    t, d = hb.shape
    n = w.shape[1]
    tm = min(tm, t)
    tn = min(tn, n)
    return pl.pallas_call(
        _proj_kernel,
        grid=(t // tm, n // tn),
        in_specs=[pl.BlockSpec((tm, d), lambda i, j: (i, 0)), pl.BlockSpec((d, tn), lambda i, j: (0, j))],
        out_specs=pl.BlockSpec((tm, tn), lambda i, j: (i, j)),
        out_shape=jax.ShapeDtypeStruct((t, n), out_dtype),
        compiler_params=_params("parallel", "arbitrary"),
        name=name,
    )(hb, w)


HALO = 16


def _conv_kernel(cur_ref, prev_ref, next_ref, w_ref, b_ref, o_ref, ext_ref, *, tm, tiles_per_seq):
    i = pl.program_id(0)
    first = (i % tiles_per_seq) == 0
    last = (i % tiles_per_seq) == tiles_per_seq - 1
    ext_ref[0:HALO, :] = jnp.where(first, 0.0, prev_ref[...].astype(F32))
    ext_ref[HALO:HALO + tm, :] = cur_ref[...].astype(F32)
    ext_ref[HALO + tm:2 * HALO + tm, :] = jnp.where(last, 0.0, next_ref[...].astype(F32))
    pad = CONV_WIDTH // 2
    acc = b_ref[...] + ext_ref[HALO - pad:HALO - pad + tm, :] * w_ref[0:1, :]
    for w in range(1, CONV_WIDTH):
        acc = acc + ext_ref[HALO - pad + w:HALO - pad + w + tm, :] * w_ref[w:w + 1, :]
    o_ref[...] = (acc * _sigmoid(acc)).astype(o_ref.dtype)


def _conv_silu(main, conv_w, conv_b, *, seq, tm=256):
    t = main.shape[0]
    tm = min(tm, seq)
    tiles_per_seq = seq // tm
    cb = 0
    r = tm // HALO
    nblk = t // HALO
    return pl.pallas_call(
        functools.partial(_conv_kernel, tm=tm, tiles_per_seq=tiles_per_seq),
        grid=(t // tm,),
        in_specs=[
            pl.BlockSpec((tm, CONV_DIM), lambda i: (i, cb)),
            pl.BlockSpec((HALO, CONV_DIM), lambda i: (jnp.maximum(i * r - 1, 0), cb)),
            pl.BlockSpec((HALO, CONV_DIM), lambda i: (jnp.minimum((i + 1) * r, nblk - 1), cb)),
            pl.BlockSpec((8, CONV_DIM), lambda i: (0, 0)),
            pl.BlockSpec((1, CONV_DIM), lambda i: (0, 0)),
        ],
        out_specs=pl.BlockSpec((tm, CONV_DIM), lambda i: (i, 0)),
        out_shape=jax.ShapeDtypeStruct((t, CONV_DIM), BF16),
        scratch_shapes=[pltpu.VMEM((tm + 2 * HALO, CONV_DIM), F32)],
        compiler_params=_params("parallel"),
        name="conv_silu",
    )(main, main, main, conv_w, conv_b)


def _split_dot(v, m01, terms):
    out = None
    r = v
    for _ in range(terms):
        part = r.astype(BF16)
        d = jnp.dot(part, m01, preferred_element_type=F32)
        out = d if out is None else out + d
        r = r - part.astype(F32)
    return out


def _ssd_kernel(xbc_ref, dt_ref, par_ref, e_ref, y_ref, h_ref):
    L = SSD_CHUNK
    d = pl.program_id(1)
    c = pl.program_id(2)

    @pl.when(c == 0)
    def _init():
        h_ref[...] = jnp.zeros_like(h_ref)

    fwd = d == 0
    pre = dt_ref[...] + par_ref[0:1, :]
    dt = jnp.maximum(pre, 0.0) + jnp.log(1.0 + jnp.exp(-jnp.abs(pre)))
    a = dt * (-jnp.exp(par_ref[1:2, :]))

    row = lax.broadcasted_iota(jnp.int32, (L, L), 0)
    col = lax.broadcasted_iota(jnp.int32, (L, L), 1)
    msk = (col - row) * (1 - 2 * d) <= 0
    tri = jnp.where(msk, 1.0, 0.0).astype(BF16)
    c_col = _tri_cumsum(tri, a)
    c_row = c_col.T
    dt_row = dt.T
    tot = jnp.where(fwd, c_col[L - 1:L, :], c_col[0:1, :])
    e_out = jnp.exp(c_col)
    f_in = dt * jnp.exp(tot - c_col)
    cd = jnp.broadcast_to(jnp.exp(tot), (8, LANES))
    stacked = jnp.concatenate([f_in, e_out, cd], axis=0)
    ex = _split_dot(stacked, e_ref[...], 2)
    f_x = ex[0:L]
    e_x = ex[L:2 * L]
    cd_x = ex[2 * L:2 * L + 1]

    n_x = SSM_HEADS * SSM_HEAD_DIM
    gn = SSM_GROUPS * SSM_STATE
    xw = (xbc_ref[:, 0:n_x].astype(F32) * f_x).astype(BF16)
    lane = lax.broadcasted_iota(jnp.int32, (L, LANES), 1)
    gw = n_x // SSM_GROUPS
    for g in range(SSM_GROUPS):
        bg = xbc_ref[:, n_x + g * SSM_STATE:n_x + (g + 1) * SSM_STATE]
        cg = xbc_ref[:, n_x + gn + g * SSM_STATE:n_x + gn + (g + 1) * SSM_STATE]
        bgt = bg.astype(F32).T.astype(BF16)
        gmat = jnp.dot(cg, bgt, preferred_element_type=F32)
        hg = h_ref[:, g * gw:(g + 1) * gw]
        yoff = jnp.dot(cg, hg.astype(BF16), preferred_element_type=F32) * e_x[:, g * gw:(g + 1) * gw]
        st = jnp.dot(bgt, xw[:, g * gw:(g + 1) * gw], preferred_element_type=F32)
        h_ref[:, g * gw:(g + 1) * gw] = cd_x[:, g * gw:(g + 1) * gw] * hg + st
        for jp in range(gw // LANES):
            lo = g * gw + jp * LANES
            xp = xbc_ref[:, lo:lo + LANES]
            outs = []
            for hh in range(2):
                h = lo // SSM_HEAD_DIM + hh
                seg = c_col[:, h:h + 1] - c_row[h:h + 1, :]
                decay = jnp.exp(jnp.where(msk, seg, -jnp.inf))
                m = (gmat * (decay * dt_row[h:h + 1, :])).astype(BF16)
                outs.append(jnp.dot(m, xp, preferred_element_type=F32))
            ydiag = jnp.where(lane < SSM_HEAD_DIM, outs[0], outs[1])
            y_ref[:, lo:lo + LANES] = (ydiag + yoff[:, jp * LANES:(jp + 1) * LANES]).astype(y_ref.dtype)


def _tri_cumsum(tri, a):
    out = None
    r = a
    for _ in range(3):
        part = r.astype(BF16)
        d = jnp.dot(tri, part, preferred_element_type=F32)
        out = d if out is None else out + d
        r = r - part.astype(F32)
    return out


def _ssd(xbc_act, dt_raw, par, expand, *, batch, seq):
    t = xbc_act.shape[0]
    L = SSD_CHUNK
    nc = seq // L
    n_x = SSM_HEADS * SSM_HEAD_DIM

    def chunk(b, d, c):
        return b * nc + c + d * (nc - 1 - 2 * c)

    return pl.pallas_call(
        _ssd_kernel,
        grid=(batch, 2, nc),
        in_specs=[
            pl.BlockSpec((L, CONV_DIM), lambda b, d, c: (chunk(b, d, c), 0)),
            pl.BlockSpec((L, LANES), lambda b, d, c: (chunk(b, d, c), d)),
            pl.BlockSpec((8, LANES), lambda b, d, c: (0, d)),
            pl.BlockSpec((LANES, n_x), lambda b, d, c: (0, 0)),
        ],
        out_specs=pl.BlockSpec((None, L, n_x), lambda b, d, c: (d, chunk(b, d, c), 0)),
        out_shape=jax.ShapeDtypeStruct((2, t, n_x), BF16),
        scratch_shapes=[pltpu.VMEM((SSM_STATE, n_x), F32)],
        compiler_params=_params("parallel", "parallel", "arbitrary"),
        name="ssd",
    )(xbc_act, dt_raw, par, expand)


STALE_MAX_LIMIT = 64.0


def _attn_block(q_ref, k_ref, v_ref, st_in, st_out, gap_ref, *, stale, tq, tk, rows, k0=0):
    m_in, l_in, acc_in = st_in
    m_out, l_out, acc_out = st_out
    if stale:
        m_bak, l_bak, acc_bak = st_out
        m_out, l_out, acc_out = st_in
    v = v_ref[pl.ds(k0, tk), :]
    n_lt = tk // LANES
    gap = None
    for rb in range(tq // rows):
        r0 = rb * rows
        for mi in range(2):
            q = q_ref[r0:r0 + rows, mi * ATT_HEAD_DIM:(mi + 1) * ATT_HEAD_DIM]
            k = k_ref[pl.ds(k0, tk), mi * ATT_HEAD_DIM:(mi + 1) * ATT_HEAD_DIM]
            s = lax.dot_general(q, k, (((1,), (1,)), ((), ())), preferred_element_type=F32)
            m_prev = m_in[mi, r0:r0 + rows, :]
            pm = s[:, 0:LANES]
            for c in range(1, n_lt):
                pm = jnp.maximum(pm, s[:, c * LANES:(c + 1) * LANES])
            m_new = jnp.maximum(m_prev, jnp.max(pm, axis=1, keepdims=True))
            alpha = jnp.exp2(m_prev - m_new)
            m_exp = m_prev if stale else m_new
            lsum = None
            ps = []
            for c in range(n_lt):
                pc = jnp.exp2(s[:, c * LANES:(c + 1) * LANES] - m_exp)
                lsum = pc if lsum is None else lsum + pc
                ps.append(pc.astype(BF16))
            pv = jnp.dot(jnp.concatenate(ps, axis=1), v, preferred_element_type=F32)
            alpha2 = jnp.concatenate([alpha] * (ATT_V_DIM // LANES), axis=1)
            if stale:
                l_prev = l_in[mi, r0:r0 + rows, :]
                acc_prev = acc_in[mi, r0:r0 + rows, :]
                m_bak[mi, r0:r0 + rows, :] = m_prev
                l_bak[mi, r0:r0 + rows, :] = l_prev
                acc_bak[mi, r0:r0 + rows, :] = acc_prev
                l_out[mi, r0:r0 + rows, :] = alpha * (l_prev + lsum)
                acc_out[mi, r0:r0 + rows, :] = alpha2 * (acc_prev + pv)
                g = pm - m_prev
                gap = g if gap is None else jnp.maximum(gap, g)
            else:
                l_out[mi, r0:r0 + rows, :] = alpha * l_in[mi, r0:r0 + rows, :] + lsum
                acc_out[mi, r0:r0 + rows, :] = alpha2 * acc_in[mi, r0:r0 + rows, :] + pv
            m_out[mi, r0:r0 + rows, :] = m_new
    if stale:
        gap_ref[...] = gap


def _attn_kernel(q_ref, k_ref, v_ref, lam_ref, nw_ref, o_ref, m_ref, l_ref, acc_ref, gap_ref,
                 *, n_kv, tq, tk, tk_exact, rows):
    kv = pl.program_id(3)
    live = (m_ref.at[0], l_ref.at[0], acc_ref.at[0])
    saved = (m_ref.at[1], l_ref.at[1], acc_ref.at[1])
    block = functools.partial(_attn_block, q_ref, k_ref, v_ref, gap_ref=gap_ref, tq=tq, rows=rows)

    @pl.when(kv == 0)
    def _init():
        for mi in range(2):
            dims = slice(mi * ATT_HEAD_DIM, (mi + 1) * ATT_HEAD_DIM)
            s0 = lax.dot_general(q_ref[:, dims], k_ref[0:LANES, dims], (((1,), (1,)), ((), ())),
                                 preferred_element_type=F32)
            m_ref[0, mi] = jnp.broadcast_to(jnp.max(s0, axis=1, keepdims=True), (tq, LANES))
        l_ref[0] = jnp.zeros(l_ref.shape[1:], F32)
        acc_ref[0] = jnp.zeros(acc_ref.shape[1:], F32)

    block(live, saved, stale=True, tk=tk)

    @pl.when(jnp.max(gap_ref[...]) > STALE_MAX_LIMIT)
    def _exact():
        for dst, src in zip(live, saved):
            dst[...] = src[...]

        def piece(i, carry):
            block(live, live, stale=False, tk=tk_exact, k0=pl.multiple_of(i * tk_exact, tk_exact))
            return carry

        lax.fori_loop(0, tk // tk_exact, piece, 0)

    @pl.when(kv == n_kv - 1)
    def _fin():
        lam = (jnp.exp(jnp.sum(lam_ref[0:1, :] * lam_ref[1:2, :], axis=1, keepdims=True))
               - jnp.exp(jnp.sum(lam_ref[2:3, :] * lam_ref[3:4, :], axis=1, keepdims=True)) + LAMBDA_INIT)
        l0 = jnp.sum(l_ref[0, 0], axis=1, keepdims=True)
        l1 = jnp.sum(l_ref[0, 1], axis=1, keepdims=True)
        o = acc_ref[0, 0] / l0 - lam * (acc_ref[0, 1] / l1)
        o = o * lax.rsqrt(jnp.mean(o * o, axis=1, keepdims=True) + NORM_EPS)
        o_ref[...] = (o * nw_ref[...] * (1.0 - LAMBDA_INIT)).astype(o_ref.dtype)


def _attention(qk, vzg, lam_par, norm_w, *, batch, seq, tq=2048, tk=2048, tk_exact=1024, rows=128):
    t = qk.shape[0]
    tq = min(tq, seq)
    tk = min(tk, seq)
    tk_exact = min(tk_exact, tk)
    rows = min(rows, tq)
    nq = seq // tq
    nk = seq // tk
    vd = ATT_V_DIM
    return pl.pallas_call(
        functools.partial(_attn_kernel, n_kv=nk, tq=tq, tk=tk, tk_exact=tk_exact, rows=rows),
        grid=(batch, ATT_HEADS, nq, nk),
        in_specs=[
            pl.BlockSpec((tq, vd), lambda b, h, i, j: (b * nq + i, QK_Q // vd + h)),
            pl.BlockSpec((tk, vd), lambda b, h, i, j: (b * nk + j, QK_K // vd + h)),
            pl.BlockSpec((tk, vd), lambda b, h, i, j: (b * nk + j, VZG_V // vd + h)),
            pl.BlockSpec((8, ATT_HEAD_DIM), lambda b, h, i, j: (0, 0)),
            pl.BlockSpec((1, vd), lambda b, h, i, j: (0, 0)),
        ],
        out_specs=pl.BlockSpec((tq, vd), lambda b, h, i, j: (b * nq + i, h)),
        out_shape=jax.ShapeDtypeStruct((t, ATT_HEADS * vd), BF16),
        scratch_shapes=[
            pltpu.VMEM((2, 2, tq, LANES), F32),
            pltpu.VMEM((2, 2, tq, LANES), F32),
            pltpu.VMEM((2, 2, tq, vd), F32),
            pltpu.VMEM((rows, LANES), F32),
        ],
        compiler_params=_params("parallel", "parallel", "parallel", "arbitrary"),
        name="diff_attention",
    )(qk, qk, vzg, lam_par, norm_w)


def _ssm_post_kernel(yf_ref, yb_ref, x_ref, z_ref, gs_ref, d_ref, nw_ref, w_ref, o_ref):
    y = yf_ref[...].astype(F32) + yb_ref[...].astype(F32) + d_ref[...] * x_ref[...].astype(F32)
    z = z_ref[...].astype(F32)
    y = y * (z * _sigmoid(z))
    gsz = y.shape[1] // SSM_GROUPS
    parts = []
    for g in range(SSM_GROUPS):
        seg = y[:, g * gsz:(g + 1) * gsz]
        parts.append(seg * lax.rsqrt(jnp.mean(seg * seg, axis=1, keepdims=True) + NORM_EPS))
    yn = jnp.concatenate(parts, axis=1) * nw_ref[...]
    ys = jnp.dot(yn.astype(BF16), w_ref[...], preferred_element_type=F32)
    o_ref[...] = _sigmoid(gs_ref[...].astype(F32)) * ys


def _ssm_post(y2, xbc_act, main, d_x, norm_w, w_br, *, tm=512):
    t = xbc_act.shape[0]
    tm = min(tm, t)
    d = D_MODEL
    row = lambda i: (i, 0)
    return pl.pallas_call(
        _ssm_post_kernel,
        grid=(t // tm,),
        in_specs=[
            pl.BlockSpec((None, tm, d), lambda i: (0, i, 0)),
            pl.BlockSpec((None, tm, d), lambda i: (1, i, 0)),
            pl.BlockSpec((tm, d), row),
            pl.BlockSpec((tm, d), lambda i: (i, VZG_Z // d)),
            pl.BlockSpec((tm, d), lambda i: (i, VZG_GS // d)),
            pl.BlockSpec((1, d), lambda i: (0, 0)),
            pl.BlockSpec((1, d), lambda i: (0, 0)),
            pl.BlockSpec((d, d), lambda i: (0, 0)),
        ],
        out_specs=pl.BlockSpec((tm, d), row),
        out_shape=jax.ShapeDtypeStruct((t, d), F32),
        compiler_params=_params("parallel"),
        name="ssm_post",
    )(y2, y2, xbc_act, main, main, d_x, norm_w, w_br)


def _mix_ln_kernel(p1_ref, oa_ref, ga_ref, h_ref, wa_ref, wo_ref, g_ref, b_ref, o_ref, ob_ref):
    ya = jnp.dot(oa_ref[...], wa_ref[...], preferred_element_type=F32)
    merged = p1_ref[...] + _sigmoid(ga_ref[...].astype(F32)) * ya
    mix = jnp.dot(merged.astype(BF16), wo_ref[...], preferred_element_type=F32)
    out = _layer_norm(ALPHA * h_ref[...] + mix, g_ref[...], b_ref[...])
    o_ref[...] = out
    ob_ref[...] = out.astype(BF16)


def _mix_ln(p1, o_att, main, h1, w_att_br, w_out, g, b, *, tm=256):
    t, d = h1.shape
    tm = min(tm, t)
    row = lambda i: (i, 0)
    const = lambda i: (0, 0)
    return pl.pallas_call(
        _mix_ln_kernel,
        grid=(t // tm,),
        in_specs=[
            pl.BlockSpec((tm, d), row),
            pl.BlockSpec((tm, d), row),
            pl.BlockSpec((tm, d), lambda i: (i, VZG_GA // d)),
            pl.BlockSpec((tm, d), row),
            pl.BlockSpec((d, d), const),
            pl.BlockSpec((d, d), const),
            pl.BlockSpec((1, d), const),
            pl.BlockSpec((1, d), const),
        ],
        out_specs=[pl.BlockSpec((tm, d), row), pl.BlockSpec((tm, d), row)],
        out_shape=[jax.ShapeDtypeStruct((t, d), F32), jax.ShapeDtypeStruct((t, d), BF16)],
        compiler_params=_params("parallel"),
        name="mix_ln",
    )(p1, o_att, main, h1, w_att_br, w_out, g, b)


def _rope_tables(seq):
    half = ROT_DIM // 2
    pos = jnp.arange(seq, dtype=F32)
    inv_freq = ROPE_THETA ** (-jnp.arange(0, ROT_DIM, 2, dtype=F32) / ROT_DIM)
    ang = pos[:, None] * inv_freq[None, :]
    cos, sin = jnp.cos(ang), jnp.sin(ang)
    zeros = jnp.zeros((seq, LANES - ROT_DIM), F32)
    zh = jnp.zeros((seq, half), F32)
    cos_t = jnp.concatenate([cos, cos, jnp.ones((seq, LANES - ROT_DIM), F32)], axis=1)
    s1_t = jnp.concatenate([zh, sin, zeros], axis=1)
    s2_t = jnp.concatenate([-sin, zh, zeros], axis=1)
    return cos_t, s1_t, s2_t


def _row(v, n=None):
    v = v.astype(F32).reshape(1, -1)
    if n is not None and v.shape[1] < n:
        v = jnp.pad(v, ((0, 0), (0, n - v.shape[1])))
    return v


def _layer(x2, batch, seq, ffn1_w_gate, ffn1_w_up, ffn1_w_down, ln1_g, ln1_b, w_in, conv_w, conv_b,
           dt_bias_f, dt_bias_b, a_log_f, a_log_b, d_skip, ssm_norm_w, w_ssm_br,
           lambda_q1, lambda_k1, lambda_q2, lambda_k2, attn_norm_w, w_att_br, w_out,
           ln2_g, ln2_b, ffn2_w_gate, ffn2_w_up, ffn2_w_down, ln3_g, ln3_b):
    d = D_MODEL
    o_z, o_xbc, o_dtf, o_dtb = 0, d, d + CONV_DIM, d + CONV_DIM + SSM_HEADS
    o_q = o_dtb + SSM_HEADS
    o_k, o_v, o_gs, o_ga = o_q + d, o_q + 2 * d, o_q + 3 * d, o_q + 4 * d
    cols = lambda lo, n: w_in[:, lo:lo + n]
    w_qk = cols(o_q, 2 * d).astype(BF16)
    w_vzg = jnp.concatenate([cols(o_v, d), cols(o_z, d), cols(o_gs, 2 * d)], axis=1).astype(BF16)
    w_xbc = cols(o_xbc, CONV_DIM).astype(BF16)
    pad = jnp.zeros((d, LANES - SSM_HEADS), w_in.dtype)
    w_dt = jnp.concatenate([cols(o_dtf, SSM_HEADS), pad, cols(o_dtb, SSM_HEADS), pad], axis=1).astype(BF16)

    h1, h1b = _ffn_ln(x2, ffn1_w_gate.astype(BF16), ffn1_w_up.astype(BF16), ffn1_w_down.astype(BF16),
                      _row(ln1_g), _row(ln1_b), emit_bf16=True)

    cos_t, s1_t, s2_t = _rope_tables(seq)
    qk = _qk_proj(h1b, w_qk, cos_t, s1_t, s2_t, seq=seq)
    vzg = _proj(h1b, w_vzg, BF16, name="vzg_proj")
    xbc = _proj(h1b, w_xbc, BF16, name="xbc_proj")
    dt_raw = _proj(h1b, w_dt, F32, name="dt_proj")

    cw = jnp.pad(conv_w.astype(F32).reshape(CONV_WIDTH, CONV_DIM), ((0, 8 - CONV_WIDTH), (0, 0)))
    xbc_act = _conv_silu(xbc, cw, _row(conv_b), seq=seq)

    par = jnp.zeros((8, 2 * LANES), F32)
    par = par.at[0, 0:SSM_HEADS].set(dt_bias_f.astype(F32)).at[0, LANES:LANES + SSM_HEADS].set(dt_bias_b.astype(F32))
    par = par.at[1, 0:SSM_HEADS].set(a_log_f.astype(F32)).at[1, LANES:LANES + SSM_HEADS].set(a_log_b.astype(F32))
    n_x = SSM_HEADS * SSM_HEAD_DIM
    expand = (jnp.arange(LANES)[:, None] == (jnp.arange(n_x)[None, :] // SSM_HEAD_DIM)).astype(BF16)
    y2 = _ssd(xbc_act, dt_raw, par, expand, batch=batch, seq=seq)

    lam_par = jnp.zeros((8, ATT_HEAD_DIM), F32)
    for r, v in enumerate((lambda_q1, lambda_k1, lambda_q2, lambda_k2)):
        lam_par = lam_par.at[r].set(v.astype(F32))
    o_att = _attention(qk, vzg, lam_par, _row(attn_norm_w), batch=batch, seq=seq)

    d_x = jnp.repeat(d_skip.astype(F32), SSM_HEAD_DIM).reshape(1, n_x)
    p1 = _ssm_post(y2, xbc_act, vzg, d_x, _row(ssm_norm_w), w_ssm_br.astype(BF16))
    h2, h2b = _mix_ln(p1, o_att, vzg, h1, w_att_br.astype(BF16), w_out.astype(BF16), _row(ln2_g), _row(ln2_b))
    del h2b
    (out,) = _ffn_ln(h2, ffn2_w_gate.astype(BF16), ffn2_w_up.astype(BF16), ffn2_w_down.astype(BF16),
                     _row(ln3_g), _row(ln3_b), emit_bf16=False)
    return out


def kernel(x, ffn1_w_gate, ffn1_w_up, ffn1_w_down, ln1_g, ln1_b, w_in, conv_w, conv_b, dt_bias_f, dt_bias_b, a_log_f, a_log_b, d_skip, ssm_norm_w, w_ssm_br, lambda_q1, lambda_k1, lambda_q2, lambda_k2, attn_norm_w, w_att_br, w_out, ln2_g, ln2_b, ffn2_w_gate, ffn2_w_up, ffn2_w_down, ln3_g, ln3_b):
    batch, seq, d = x.shape
    assert d == D_MODEL and seq % SSD_CHUNK == 0
    params = (ffn1_w_gate, ffn1_w_up, ffn1_w_down, ln1_g, ln1_b, w_in, conv_w, conv_b, dt_bias_f, dt_bias_b,
              a_log_f, a_log_b, d_skip, ssm_norm_w, w_ssm_br, lambda_q1, lambda_k1, lambda_q2, lambda_k2,
              attn_norm_w, w_att_br, w_out, ln2_g, ln2_b, ffn2_w_gate, ffn2_w_up, ffn2_w_down, ln3_g, ln3_b)
    assert all(p.shape[0] == DEPTH for p in params)
    out = _layer(x.reshape(batch * seq, d), batch, seq, *[p[0] for p in params])
    return out.reshape(batch, seq, d).astype(x.dtype)
```

```python
import functools
import math

import jax
import jax.numpy as jnp
from jax import lax
from jax.experimental import pallas as pl
from jax.experimental.pallas import tpu as pltpu

F32 = jnp.float32
BF16 = jnp.bfloat16

D_MODEL = 2048
SSM_HEAD_DIM = 64
SSM_HEADS = 32
SSM_GROUPS = 4
SSM_STATE = 128
CONV_WIDTH = 5
CONV_DIM = 3072
SSD_CHUNK = 128
ATT_HEADS = 8
ATT_HEAD_DIM = 128
ATT_V_DIM = 256
ROPE_THETA = 500000.0
ROT_DIM = 32
D_FF = 5632
DEPTH = 1
ALPHA = (2 * DEPTH) ** 0.25
NORM_EPS = 1e-5
LAMBDA_INIT = 0.8 - 0.6 * math.exp(-0.3 * 0)

LANES = 128
VMEM_LIMIT = 56 * 2 ** 20

QK_Q, QK_K = 0, 2048
VZG_V, VZG_Z, VZG_GS, VZG_GA = 0, 2048, 4096, 6144
Q_SCALE = ATT_HEAD_DIM ** -0.5 * math.log2(math.e)


def _params(*sem):
    return pltpu.CompilerParams(dimension_semantics=sem, vmem_limit_bytes=VMEM_LIMIT)


def _layer_norm(y, g, b):
    mu = jnp.mean(y, axis=-1, keepdims=True)
    yc = y - mu
    var = jnp.mean(yc * yc, axis=-1, keepdims=True)
    return yc * lax.rsqrt(var + NORM_EPS) * g + b


def _sigmoid(v):
    return 1.0 / (1.0 + jnp.exp(-v))


def _ffn_ln_kernel(x_ref, wg_ref, wu_ref, wd_ref, g_ref, b_ref, *rest, n_f, emit_bf16):
    if emit_bf16:
        o_ref, ob_ref, xb_ref, acc_ref = rest
    else:
        o_ref, xb_ref, acc_ref = rest
    f = pl.program_id(1)

    @pl.when(f == 0)
    def _init():
        xb_ref[...] = x_ref[...].astype(BF16)
        acc_ref[...] = jnp.zeros_like(acc_ref)

    xb = xb_ref[...]
    hg = jnp.dot(xb, wg_ref[...], preferred_element_type=F32)
    hu = jnp.dot(xb, wu_ref[...], preferred_element_type=F32)
    h = hg * _sigmoid(hg) * hu
    acc_ref[...] += jnp.dot(h.astype(BF16), wd_ref[...], preferred_element_type=F32)

    @pl.when(f == n_f - 1)
    def _fin():
        y = ALPHA * x_ref[...] + 0.5 * acc_ref[...]
        out = _layer_norm(y, g_ref[...], b_ref[...])
        o_ref[...] = out
        if emit_bf16:
            ob_ref[...] = out.astype(BF16)


def _ffn_ln(x, wg, wu, wd, g, b, *, emit_bf16, tm=512, tf=512):
    t, d = x.shape
    dff = wg.shape[1]
    tm = min(tm, t)
    n_f = dff // tf
    out_shape = [jax.ShapeDtypeStruct((t, d), F32)]
    out_specs = [pl.BlockSpec((tm, d), lambda i, f: (i, 0))]
    if emit_bf16:
        out_shape.append(jax.ShapeDtypeStruct((t, d), BF16))
        out_specs.append(pl.BlockSpec((tm, d), lambda i, f: (i, 0)))
    return pl.pallas_call(
        functools.partial(_ffn_ln_kernel, n_f=n_f, emit_bf16=emit_bf16),
        grid=(t // tm, n_f),
        in_specs=[
            pl.BlockSpec((tm, d), lambda i, f: (i, 0)),
            pl.BlockSpec((d, tf), lambda i, f: (0, f)),
            pl.BlockSpec((d, tf), lambda i, f: (0, f)),
            pl.BlockSpec((tf, d), lambda i, f: (f, 0)),
            pl.BlockSpec((1, d), lambda i, f: (0, 0)),
            pl.BlockSpec((1, d), lambda i, f: (0, 0)),
        ],
        out_specs=out_specs,
        out_shape=out_shape,
        scratch_shapes=[pltpu.VMEM((tm, d), BF16), pltpu.VMEM((tm, d), F32)],
        compiler_params=_params("parallel", "arbitrary"),
        name="ffn_ln",
    )(x, wg, wu, wd, g, b)


def _qk_proj_kernel(x_ref, w_ref, cos_ref, s1_ref, s2_ref, o_ref, *, n_q, tn):
    j = pl.program_id(1)
    y = jnp.dot(x_ref[...], w_ref[...], preferred_element_type=F32)
    sc = jnp.where(j < n_q, Q_SCALE, 1.0).astype(F32)
    c = cos_ref[...] * sc
    s1 = s1_ref[...] * sc
    s2 = s2_ref[...] * sc
    half = ROT_DIM // 2
    for k in range(tn // LANES):
        yk = y[:, k * LANES:(k + 1) * LANES]
        r = yk * c + pltpu.roll(yk, half, 1) * s1 + pltpu.roll(yk, LANES - half, 1) * s2
        o_ref[:, k * LANES:(k + 1) * LANES] = r.astype(o_ref.dtype)


def _qk_proj(hb, w_qk, cos_t, s1_t, s2_t, *, seq, tm=1024, tn=1024):
    t, d = hb.shape
    n = w_qk.shape[1]
    tm = min(tm, seq)
    tiles_per_seq = seq // tm
    tab = pl.BlockSpec((tm, LANES), lambda i, j: (i % tiles_per_seq, 0))
    return pl.pallas_call(
        functools.partial(_qk_proj_kernel, n_q=QK_K // tn, tn=tn),
        grid=(t // tm, n // tn),
        in_specs=[
            pl.BlockSpec((tm, d), lambda i, j: (i, 0)),
            pl.BlockSpec((d, tn), lambda i, j: (0, j)),
            tab, tab, tab,
        ],
        out_specs=pl.BlockSpec((tm, tn), lambda i, j: (i, j)),
        out_shape=jax.ShapeDtypeStruct((t, n), BF16),
        compiler_params=_params("parallel", "arbitrary"),
        name="qk_proj",
    )(hb, w_qk, cos_t, s1_t, s2_t)


def _proj_kernel(x_ref, w_ref, o_ref):
    o_ref[...] = jnp.dot(x_ref[...], w_ref[...], preferred_element_type=F32).astype(o_ref.dtype)


def _proj(hb, w, out_dtype, *, name, tm=1024, tn=1024):
    t, d = hb.shape
    n = w.shape[1]
    tm = min(tm, t)
    tn = min(tn, n)
    return pl.pallas_call(
        _proj_kernel,
        grid=(t // tm, n // tn),
        in_specs=[pl.BlockSpec((tm, d), lambda i, j: (i, 0)), pl.BlockSpec((d, tn), lambda i, j: (0, j))],
        out_specs=pl.BlockSpec((tm, tn), lambda i, j: (i, j)),
        out_shape=jax.ShapeDtypeStruct((t, n), out_dtype),
        compiler_params=_params("parallel", "arbitrary"),
        name=name,
    )(hb, w)


HALO = 64
CONV_ROWS = 128


def _conv_kernel(cur_ref, prev_ref, next_ref, sh_ref, w_ref, b_ref, o_ref, ext_ref, *, tm, tiles_per_seq):
    i = pl.program_id(0)
    first = (i % tiles_per_seq) == 0
    last = (i % tiles_per_seq) == tiles_per_seq - 1
    ext_ref[0:HALO, :] = jnp.where(first, 0.0, prev_ref[...]).astype(BF16)
    ext_ref[HALO:HALO + tm, :] = cur_ref[...]
    ext_ref[HALO + tm:2 * HALO + tm, :] = jnp.where(last, 0.0, next_ref[...]).astype(BF16)
    pad = CONV_WIDTH // 2
    for rb in range(tm // CONV_ROWS):
        r0 = rb * CONV_ROWS
        win = ext_ref[r0:r0 + CONV_ROWS + 2 * HALO, :]
        acc = b_ref[...] + ext_ref[r0 + HALO:r0 + HALO + CONV_ROWS, :].astype(F32) * w_ref[pad:pad + 1, :]
        for w in range(CONV_WIDTH):
            if w != pad:
                shifted = jnp.dot(sh_ref[w], win, preferred_element_type=F32)
                acc = acc + shifted * w_ref[w:w + 1, :]
        o_ref[r0:r0 + CONV_ROWS, :] = (acc * _sigmoid(acc)).astype(o_ref.dtype)


def _conv_silu(xbc, shifts, conv_w, conv_b, *, seq, tm=256):
    t = xbc.shape[0]
    tm = min(tm, seq)
    tiles_per_seq = seq // tm
    r = tm // HALO
    nblk = t // HALO
    return pl.pallas_call(
        functools.partial(_conv_kernel, tm=tm, tiles_per_seq=tiles_per_seq),
        grid=(t // tm,),
        in_specs=[
            pl.BlockSpec((tm, CONV_DIM), lambda i: (i, 0)),
            pl.BlockSpec((HALO, CONV_DIM), lambda i: (jnp.maximum(i * r - 1, 0), 0)),
            pl.BlockSpec((HALO, CONV_DIM), lambda i: (jnp.minimum((i + 1) * r, nblk - 1), 0)),
            pl.BlockSpec((CONV_WIDTH, CONV_ROWS, CONV_ROWS + 2 * HALO), lambda i: (0, 0, 0)),
            pl.BlockSpec((8, CONV_DIM), lambda i: (0, 0)),
            pl.BlockSpec((1, CONV_DIM), lambda i: (0, 0)),
        ],
        out_specs=pl.BlockSpec((tm, CONV_DIM), lambda i: (i, 0)),
        out_shape=jax.ShapeDtypeStruct((t, CONV_DIM), BF16),
        scratch_shapes=[pltpu.VMEM((tm + 2 * HALO, CONV_DIM), BF16)],
        compiler_params=_params("parallel"),
        name="conv_silu",
    )(xbc, xbc, xbc, shifts, conv_w, conv_b)


def _split_dot(v, m01, terms):
    out = None
    r = v
    for _ in range(terms):
        part = r.astype(BF16)
        d = jnp.dot(part, m01, preferred_element_type=F32)
        out = d if out is None else out + d
        r = r - part.astype(F32)
    return out


def _ssd_kernel(xbc_ref, dt_ref, par_ref, e_ref, y_ref, h_ref):
    L = SSD_CHUNK
    d = pl.program_id(1)
    c = pl.program_id(2)

    @pl.when(c == 0)
    def _init():
        h_ref[...] = jnp.zeros_like(h_ref)

    fwd = d == 0
    pre = dt_ref[...] + par_ref[0:1, :]
    dt = jnp.maximum(pre, 0.0) + jnp.log(1.0 + jnp.exp(-jnp.abs(pre)))
    a = dt * (-jnp.exp(par_ref[1:2, :]))

    row = lax.broadcasted_iota(jnp.int32, (L, L), 0)
    col = lax.broadcasted_iota(jnp.int32, (L, L), 1)
    msk = (col - row) * (1 - 2 * d) <= 0
    tri = jnp.where(msk, 1.0, 0.0).astype(BF16)
    c_col = _tri_cumsum(tri, a)
    c_row = c_col.T
    dt_row = dt.T
    tot = jnp.where(fwd, c_col[L - 1:L, :], c_col[0:1, :])
    e_out = jnp.exp(c_col)
    f_in = dt * jnp.exp(tot - c_col)
    cd = jnp.broadcast_to(jnp.exp(tot), (8, LANES))
    stacked = jnp.concatenate([f_in, e_out, cd], axis=0)
    ex = _split_dot(stacked, e_ref[...], 2)
    f_x = ex[0:L]
    e_x = ex[L:2 * L]
    cd_x = ex[2 * L:2 * L + 1]

    n_x = SSM_HEADS * SSM_HEAD_DIM
    gn = SSM_GROUPS * SSM_STATE
    xw = (xbc_ref[:, 0:n_x].astype(F32) * f_x).astype(BF16)
    lane = lax.broadcasted_iota(jnp.int32, (L, LANES), 1)
    gw = n_x // SSM_GROUPS
    for g in range(SSM_GROUPS):
        bg = xbc_ref[:, n_x + g * SSM_STATE:n_x + (g + 1) * SSM_STATE]
        cg = xbc_ref[:, n_x + gn + g * SSM_STATE:n_x + gn + (g + 1) * SSM_STATE]
        bgt = bg.astype(F32).T.astype(BF16)
        gmat = jnp.dot(cg, bgt, preferred_element_type=F32)
        hg = h_ref[:, g * gw:(g + 1) * gw]
        yoff = jnp.dot(cg, hg.astype(BF16), preferred_element_type=F32) * e_x[:, g * gw:(g + 1) * gw]
        st = jnp.dot(bgt, xw[:, g * gw:(g + 1) * gw], preferred_element_type=F32)
        h_ref[:, g * gw:(g + 1) * gw] = cd_x[:, g * gw:(g + 1) * gw] * hg + st
        for jp in range(gw // LANES):
            lo = g * gw + jp * LANES
            xp = xbc_ref[:, lo:lo + LANES]
            outs = []
            for hh in range(2):
                h = lo // SSM_HEAD_DIM + hh
                seg = c_col[:, h:h + 1] - c_row[h:h + 1, :]
                decay = jnp.exp(jnp.where(msk, seg, -jnp.inf))
                m = (gmat * (decay * dt_row[h:h + 1, :])).astype(BF16)
                outs.append(jnp.dot(m, xp, preferred_element_type=F32))
            ydiag = jnp.where(lane < SSM_HEAD_DIM, outs[0], outs[1])
            y_ref[:, lo:lo + LANES] = (ydiag + yoff[:, jp * LANES:(jp + 1) * LANES]).astype(y_ref.dtype)


def _tri_cumsum(tri, a):
    out = None
    r = a
    for _ in range(3):
        part = r.astype(BF16)
        d = jnp.dot(tri, part, preferred_element_type=F32)
        out = d if out is None else out + d
        r = r - part.astype(F32)
    return out


def _ssd(xbc_act, dt_raw, par, expand, *, batch, seq):
    t = xbc_act.shape[0]
    L = SSD_CHUNK
    nc = seq // L
    n_x = SSM_HEADS * SSM_HEAD_DIM

    def chunk(b, d, c):
        return b * nc + c + d * (nc - 1 - 2 * c)

    return pl.pallas_call(
        _ssd_kernel,
        grid=(batch, 2, nc),
        in_specs=[
            pl.BlockSpec((L, CONV_DIM), lambda b, d, c: (chunk(b, d, c), 0)),
            pl.BlockSpec((L, LANES), lambda b, d, c: (chunk(b, d, c), d)),
            pl.BlockSpec((8, LANES), lambda b, d, c: (0, d)),
            pl.BlockSpec((LANES, n_x), lambda b, d, c: (0, 0)),
        ],
        out_specs=pl.BlockSpec((None, L, n_x), lambda b, d, c: (d, chunk(b, d, c), 0)),
        out_shape=jax.ShapeDtypeStruct((2, t, n_x), BF16),
        scratch_shapes=[pltpu.VMEM((SSM_STATE, n_x), F32)],
        compiler_params=_params("parallel", "parallel", "arbitrary"),
        name="ssd",
    )(xbc_act, dt_raw, par, expand)


STALE_MAX_LIMIT = 64.0


def _attn_block(q_ref, k_ref, v_ref, st_in, st_out, gap_ref, *, stale, tq, tk, rows, k0=0):
    m_in, l_in, acc_in = st_in
    m_out, l_out, acc_out = st_out
    if stale:
        m_bak, l_bak, acc_bak = st_out
        m_out, l_out, acc_out = st_in
    v = v_ref[pl.ds(k0, tk), :]
    n_lt = tk // LANES
    gap = None
    for rb in range(tq // rows):
        r0 = rb * rows
        for mi in range(2):
            q = q_ref[r0:r0 + rows, mi * ATT_HEAD_DIM:(mi + 1) * ATT_HEAD_DIM]
            k = k_ref[pl.ds(k0, tk), mi * ATT_HEAD_DIM:(mi + 1) * ATT_HEAD_DIM]
            s = lax.dot_general(q, k, (((1,), (1,)), ((), ())), preferred_element_type=F32)
            m_prev = m_in[mi, r0:r0 + rows, :]
            pm = s[:, 0:LANES]
            for c in range(1, n_lt):
                pm = jnp.maximum(pm, s[:, c * LANES:(c + 1) * LANES])
            m_new = jnp.maximum(m_prev, jnp.max(pm, axis=1, keepdims=True))
            alpha = jnp.exp2(m_prev - m_new)
            m_exp = m_prev if stale else m_new
            lsum = None
            ps = []
            for c in range(n_lt):
                pc = jnp.exp2(s[:, c * LANES:(c + 1) * LANES] - m_exp)
                lsum = pc if lsum is None else lsum + pc
                ps.append(pc.astype(BF16))
            pv = jnp.dot(jnp.concatenate(ps, axis=1), v, preferred_element_type=F32)
            alpha2 = jnp.concatenate([alpha] * (ATT_V_DIM // LANES), axis=1)
            if stale:
                l_prev = l_in[mi, r0:r0 + rows, :]
                acc_prev = acc_in[mi, r0:r0 + rows, :]
                m_bak[mi, r0:r0 + rows, :] = m_prev
                l_bak[mi, r0:r0 + rows, :] = l_prev
                acc_bak[mi, r0:r0 + rows, :] = acc_prev
                l_out[mi, r0:r0 + rows, :] = alpha * (l_prev + lsum)
                acc_out[mi, r0:r0 + rows, :] = alpha2 * (acc_prev + pv)
                g = pm - m_prev
                gap = g if gap is None else jnp.maximum(gap, g)
            else:
                l_out[mi, r0:r0 + rows, :] = alpha * l_in[mi, r0:r0 + rows, :] + lsum
                acc_out[mi, r0:r0 + rows, :] = alpha2 * acc_in[mi, r0:r0 + rows, :] + pv
            m_out[mi, r0:r0 + rows, :] = m_new
    if stale:
        gap_ref[...] = gap


def _attn_kernel(q_ref, k_ref, v_ref, lam_ref, nw_ref, o_ref, m_ref, l_ref, acc_ref, gap_ref,
                 *, n_kv, tq, tk, tk_exact, rows):
    kv = pl.program_id(3)
    live = (m_ref.at[0], l_ref.at[0], acc_ref.at[0])
    saved = (m_ref.at[1], l_ref.at[1], acc_ref.at[1])
    block = functools.partial(_attn_block, q_ref, k_ref, v_ref, gap_ref=gap_ref, tq=tq, rows=rows)

    @pl.when(kv == 0)
    def _init():
        for mi in range(2):
            dims = slice(mi * ATT_HEAD_DIM, (mi + 1) * ATT_HEAD_DIM)
            s0 = lax.dot_general(q_ref[:, dims], k_ref[0:LANES, dims], (((1,), (1,)), ((), ())),
                                 preferred_element_type=F32)
            m_ref[0, mi] = jnp.broadcast_to(jnp.max(s0, axis=1, keepdims=True), (tq, LANES))
        l_ref[0] = jnp.zeros(l_ref.shape[1:], F32)
        acc_ref[0] = jnp.zeros(acc_ref.shape[1:], F32)

    block(live, saved, stale=True, tk=tk)

    @pl.when(jnp.max(gap_ref[...]) > STALE_MAX_LIMIT)
    def _exact():
        for dst, src in zip(live, saved):
            dst[...] = src[...]

        def piece(i, carry):
            block(live, live, stale=False, tk=tk_exact, k0=pl.multiple_of(i * tk_exact, tk_exact))
            return carry

        lax.fori_loop(0, tk // tk_exact, piece, 0)

    @pl.when(kv == n_kv - 1)
    def _fin():
        lam = (jnp.exp(jnp.sum(lam_ref[0:1, :] * lam_ref[1:2, :], axis=1, keepdims=True))
               - jnp.exp(jnp.sum(lam_ref[2:3, :] * lam_ref[3:4, :], axis=1, keepdims=True)) + LAMBDA_INIT)
        l0 = jnp.sum(l_ref[0, 0], axis=1, keepdims=True)
        l1 = jnp.sum(l_ref[0, 1], axis=1, keepdims=True)
        o = acc_ref[0, 0] / l0 - lam * (acc_ref[0, 1] / l1)
        o = o * lax.rsqrt(jnp.mean(o * o, axis=1, keepdims=True) + NORM_EPS)
        o_ref[...] = (o * nw_ref[...] * (1.0 - LAMBDA_INIT)).astype(o_ref.dtype)


def _attention(qk, vzg, lam_par, norm_w, *, batch, seq, tq=2048, tk=2048, tk_exact=1024, rows=128):
    t = qk.shape[0]
    tq = min(tq, seq)
    tk = min(tk, seq)
    tk_exact = min(tk_exact, tk)
    rows = min(rows, tq)
    nq = seq // tq
    nk = seq // tk
    vd = ATT_V_DIM
    return pl.pallas_call(
        functools.partial(_attn_kernel, n_kv=nk, tq=tq, tk=tk, tk_exact=tk_exact, rows=rows),
        grid=(batch, ATT_HEADS, nq, nk),
        in_specs=[
            pl.BlockSpec((tq, vd), lambda b, h, i, j: (b * nq + i, QK_Q // vd + h)),
            pl.BlockSpec((tk, vd), lambda b, h, i, j: (b * nk + j, QK_K // vd + h)),
            pl.BlockSpec((tk, vd), lambda b, h, i, j: (b * nk + j, VZG_V // vd + h)),
            pl.BlockSpec((8, ATT_HEAD_DIM), lambda b, h, i, j: (0, 0)),
            pl.BlockSpec((1, vd), lambda b, h, i, j: (0, 0)),
        ],
        out_specs=pl.BlockSpec((tq, vd), lambda b, h, i, j: (b * nq + i, h)),
        out_shape=jax.ShapeDtypeStruct((t, ATT_HEADS * vd), BF16),
        scratch_shapes=[
            pltpu.VMEM((2, 2, tq, LANES), F32),
            pltpu.VMEM((2, 2, tq, LANES), F32),
            pltpu.VMEM((2, 2, tq, vd), F32),
            pltpu.VMEM((rows, LANES), F32),
        ],
        compiler_params=_params("parallel", "parallel", "parallel", "arbitrary"),
        name="diff_attention",
    )(qk, qk, vzg, lam_par, norm_w)


def _ssm_post_kernel(yf_ref, yb_ref, x_ref, z_ref, gs_ref, d_ref, nw_ref, w_ref, o_ref):
    y = yf_ref[...].astype(F32) + yb_ref[...].astype(F32) + d_ref[...] * x_ref[...].astype(F32)
    z = z_ref[...].astype(F32)
    y = y * (z * _sigmoid(z))
    gsz = y.shape[1] // SSM_GROUPS
    parts = []
    for g in range(SSM_GROUPS):
        seg = y[:, g * gsz:(g + 1) * gsz]
        parts.append(seg * lax.rsqrt(jnp.mean(seg * seg, axis=1, keepdims=True) + NORM_EPS))
    yn = jnp.concatenate(parts, axis=1) * nw_ref[...]
    ys = jnp.dot(yn.astype(BF16), w_ref[...], preferred_element_type=F32)
    o_ref[...] = _sigmoid(gs_ref[...].astype(F32)) * ys


def _ssm_post(y2, xbc_act, main, d_x, norm_w, w_br, *, tm=512):
    t = xbc_act.shape[0]
    tm = min(tm, t)
    d = D_MODEL
    row = lambda i: (i, 0)
    return pl.pallas_call(
        _ssm_post_kernel,
        grid=(t // tm,),
        in_specs=[
            pl.BlockSpec((None, tm, d), lambda i: (0, i, 0)),
            pl.BlockSpec((None, tm, d), lambda i: (1, i, 0)),
            pl.BlockSpec((tm, d), row),
            pl.BlockSpec((tm, d), lambda i: (i, VZG_Z // d)),
            pl.BlockSpec((tm, d), lambda i: (i, VZG_GS // d)),
            pl.BlockSpec((1, d), lambda i: (0, 0)),
            pl.BlockSpec((1, d), lambda i: (0, 0)),
            pl.BlockSpec((d, d), lambda i: (0, 0)),
        ],
        out_specs=pl.BlockSpec((tm, d), row),
        out_shape=jax.ShapeDtypeStruct((t, d), F32),
        compiler_params=_params("parallel"),
        name="ssm_post",
    )(y2, y2, xbc_act, main, main, d_x, norm_w, w_br)


def _mix_ln_kernel(p1_ref, oa_ref, ga_ref, h_ref, wa_ref, wo_ref, g_ref, b_ref, o_ref, ob_ref):
    ya = jnp.dot(oa_ref[...], wa_ref[...], preferred_element_type=F32)
    merged = p1_ref[...] + _sigmoid(ga_ref[...].astype(F32)) * ya
    mix = jnp.dot(merged.astype(BF16), wo_ref[...], preferred_element_type=F32)
    out = _layer_norm(ALPHA * h_ref[...] + mix, g_ref[...], b_ref[...])
    o_ref[...] = out
    ob_ref[...] = out.astype(BF16)


def _mix_ln(p1, o_att, main, h1, w_att_br, w_out, g, b, *, tm=256):
    t, d = h1.shape
    tm = min(tm, t)
    row = lambda i: (i, 0)
    const = lambda i: (0, 0)
    return pl.pallas_call(
        _mix_ln_kernel,
        grid=(t // tm,),
        in_specs=[
            pl.BlockSpec((tm, d), row),
            pl.BlockSpec((tm, d), row),
            pl.BlockSpec((tm, d), lambda i: (i, VZG_GA // d)),
            pl.BlockSpec((tm, d), row),
            pl.BlockSpec((d, d), const),
            pl.BlockSpec((d, d), const),
            pl.BlockSpec((1, d), const),
            pl.BlockSpec((1, d), const),
        ],
        out_specs=[pl.BlockSpec((tm, d), row), pl.BlockSpec((tm, d), row)],
        out_shape=[jax.ShapeDtypeStruct((t, d), F32), jax.ShapeDtypeStruct((t, d), BF16)],
        compiler_params=_params("parallel"),
        name="mix_ln",
    )(p1, o_att, main, h1, w_att_br, w_out, g, b)


def _rope_tables(seq):
    half = ROT_DIM // 2
    pos = jnp.arange(seq, dtype=F32)
    inv_freq = ROPE_THETA ** (-jnp.arange(0, ROT_DIM, 2, dtype=F32) / ROT_DIM)
    ang = pos[:, None] * inv_freq[None, :]
    cos, sin = jnp.cos(ang), jnp.sin(ang)
    zeros = jnp.zeros((seq, LANES - ROT_DIM), F32)
    zh = jnp.zeros((seq, half), F32)
    cos_t = jnp.concatenate([cos, cos, jnp.ones((seq, LANES - ROT_DIM), F32)], axis=1)
    s1_t = jnp.concatenate([zh, sin, zeros], axis=1)
    s2_t = jnp.concatenate([-sin, zh, zeros], axis=1)
    return cos_t, s1_t, s2_t


def _row(v, n=None):
    v = v.astype(F32).reshape(1, -1)
    if n is not None and v.shape[1] < n:
        v = jnp.pad(v, ((0, 0), (0, n - v.shape[1])))
    return v


def _layer(x2, batch, seq, ffn1_w_gate, ffn1_w_up, ffn1_w_down, ln1_g, ln1_b, w_in, conv_w, conv_b,
           dt_bias_f, dt_bias_b, a_log_f, a_log_b, d_skip, ssm_norm_w, w_ssm_br,
           lambda_q1, lambda_k1, lambda_q2, lambda_k2, attn_norm_w, w_att_br, w_out,
           ln2_g, ln2_b, ffn2_w_gate, ffn2_w_up, ffn2_w_down, ln3_g, ln3_b):
    d = D_MODEL
    o_z, o_xbc, o_dtf, o_dtb = 0, d, d + CONV_DIM, d + CONV_DIM + SSM_HEADS
    o_q = o_dtb + SSM_HEADS
    o_k, o_v, o_gs, o_ga = o_q + d, o_q + 2 * d, o_q + 3 * d, o_q + 4 * d
    cols = lambda lo, n: w_in[:, lo:lo + n]
    w_qk = cols(o_q, 2 * d).astype(BF16)
    w_vzg = jnp.concatenate([cols(o_v, d), cols(o_z, d), cols(o_gs, 2 * d)], axis=1).astype(BF16)
    w_xbc = cols(o_xbc, CONV_DIM).astype(BF16)
    pad = jnp.zeros((d, LANES - SSM_HEADS), w_in.dtype)
    w_dt = jnp.concatenate([cols(o_dtf, SSM_HEADS), pad, cols(o_dtb, SSM_HEADS), pad], axis=1).astype(BF16)

    h1, h1b = _ffn_ln(x2, ffn1_w_gate.astype(BF16), ffn1_w_up.astype(BF16), ffn1_w_down.astype(BF16),
                      _row(ln1_g), _row(ln1_b), emit_bf16=True)

    cos_t, s1_t, s2_t = _rope_tables(seq)
    qk = _qk_proj(h1b, w_qk, cos_t, s1_t, s2_t, seq=seq)
    vzg = _proj(h1b, w_vzg, BF16, name="vzg_proj")
    xbc = _proj(h1b, w_xbc, BF16, name="xbc_proj")
    dt_raw = _proj(h1b, w_dt, F32, name="dt_proj")

    cw = jnp.pad(conv_w.astype(F32).reshape(CONV_WIDTH, CONV_DIM), ((0, 8 - CONV_WIDTH), (0, 0)))
    win_col = jnp.arange(CONV_ROWS + 2 * HALO)[None, None, :]
    src_col = jnp.arange(CONV_ROWS)[None, :, None] + HALO - CONV_WIDTH // 2 + jnp.arange(CONV_WIDTH)[:, None, None]
    shifts = (win_col == src_col).astype(BF16)
    xbc_act = _conv_silu(xbc, shifts, cw, _row(conv_b), seq=seq)

    par = jnp.zeros((8, 2 * LANES), F32)
    par = par.at[0, 0:SSM_HEADS].set(dt_bias_f.astype(F32)).at[0, LANES:LANES + SSM_HEADS].set(dt_bias_b.astype(F32))
    par = par.at[1, 0:SSM_HEADS].set(a_log_f.astype(F32)).at[1, LANES:LANES + SSM_HEADS].set(a_log_b.astype(F32))
    n_x = SSM_HEADS * SSM_HEAD_DIM
    expand = (jnp.arange(LANES)[:, None] == (jnp.arange(n_x)[None, :] // SSM_HEAD_DIM)).astype(BF16)
    y2 = _ssd(xbc_act, dt_raw, par, expand, batch=batch, seq=seq)

    lam_par = jnp.zeros((8, ATT_HEAD_DIM), F32)
    for r, v in enumerate((lambda_q1, lambda_k1, lambda_q2, lambda_k2)):
        lam_par = lam_par.at[r].set(v.astype(F32))
    o_att = _attention(qk, vzg, lam_par, _row(attn_norm_w), batch=batch, seq=seq)

    d_x = jnp.repeat(d_skip.astype(F32), SSM_HEAD_DIM).reshape(1, n_x)
    p1 = _ssm_post(y2, xbc_act, vzg, d_x, _row(ssm_norm_w), w_ssm_br.astype(BF16))
    h2, h2b = _mix_ln(p1, o_att, vzg, h1, w_att_br.astype(BF16), w_out.astype(BF16), _row(ln2_g), _row(ln2_b))
    del h2b
    (out,) = _ffn_ln(h2, ffn2_w_gate.astype(BF16), ffn2_w_up.astype(BF16), ffn2_w_down.astype(BF16),
                     _row(ln3_g), _row(ln3_b), emit_bf16=False)
    return out


def kernel(x, ffn1_w_gate, ffn1_w_up, ffn1_w_down, ln1_g, ln1_b, w_in, conv_w, conv_b, dt_bias_f, dt_bias_b, a_log_f, a_log_b, d_skip, ssm_norm_w, w_ssm_br, lambda_q1, lambda_k1, lambda_q2, lambda_k2, attn_norm_w, w_att_br, w_out, ln2_g, ln2_b, ffn2_w_gate, ffn2_w_up, ffn2_w_down, ln3_g, ln3_b):
    batch, seq, d = x.shape
    assert d == D_MODEL and seq % SSD_CHUNK == 0
    params = (ffn1_w_gate, ffn1_w_up, ffn1_w_down, ln1_g, ln1_b, w_in, conv_w, conv_b, dt_bias_f, dt_bias_b,
              a_log_f, a_log_b, d_skip, ssm_norm_w, w_ssm_br, lambda_q1, lambda_k1, lambda_q2, lambda_k2,
              attn_norm_w, w_att_br, w_out, ln2_g, ln2_b, ffn2_w_gate, ffn2_w_up, ffn2_w_down, ln3_g, ln3_b)
    assert all(p.shape[0] == DEPTH for p in params)
    out = _layer(x.reshape(batch * seq, d), batch, seq, *[p[0] for p in params])
    return out.reshape(batch, seq, d).astype(x.dtype)
```

```python
import functools
import math

import jax
import jax.numpy as jnp
from jax import lax
from jax.experimental import pallas as pl
from jax.experimental.pallas import tpu as pltpu

F32 = jnp.float32
BF16 = jnp.bfloat16

D_MODEL = 2048
SSM_HEAD_DIM = 64
SSM_HEADS = 32
SSM_GROUPS = 4
SSM_STATE = 128
CONV_WIDTH = 5
CONV_DIM = 3072
SSD_CHUNK = 128
ATT_HEADS = 8
ATT_HEAD_DIM = 128
ATT_V_DIM = 256
ROPE_THETA = 500000.0
ROT_DIM = 32
D_FF = 5632
DEPTH = 1
ALPHA = (2 * DEPTH) ** 0.25
NORM_EPS = 1e-5
LAMBDA_INIT = 0.8 - 0.6 * math.exp(-0.3 * 0)

LANES = 128
VMEM_LIMIT = 56 * 2 ** 20

QK_Q, QK_K = 0, 2048
VZG_V, VZG_Z, VZG_GS, VZG_GA = 0, 2048, 4096, 6144
Q_SCALE = ATT_HEAD_DIM ** -0.5 * math.log2(math.e)


def _params(*sem):
    return pltpu.CompilerParams(dimension_semantics=sem, vmem_limit_bytes=VMEM_LIMIT)


def _layer_norm(y, g, b):
    mu = jnp.mean(y, axis=-1, keepdims=True)
    yc = y - mu
    var = jnp.mean(yc * yc, axis=-1, keepdims=True)
    return yc * lax.rsqrt(var + NORM_EPS) * g + b


def _sigmoid(v):
    return 1.0 / (1.0 + jnp.exp(-v))


def _ffn_ln_kernel(x_ref, wg_ref, wu_ref, wd_ref, g_ref, b_ref, *rest, n_f, emit_bf16):
    if emit_bf16:
        o_ref, ob_ref, xb_ref, acc_ref = rest
    else:
        o_ref, xb_ref, acc_ref = rest
    f = pl.program_id(1)

    @pl.when(f == 0)
    def _init():
        xb_ref[...] = x_ref[...].astype(BF16)
        acc_ref[...] = jnp.zeros_like(acc_ref)

    xb = xb_ref[...]
    hg = jnp.dot(xb, wg_ref[...], preferred_element_type=F32)
    hu = jnp.dot(xb, wu_ref[...], preferred_element_type=F32)
    h = hg * _sigmoid(hg) * hu
    acc_ref[...] += jnp.dot(h.astype(BF16), wd_ref[...], preferred_element_type=F32)

    @pl.when(f == n_f - 1)
    def _fin():
        y = ALPHA * x_ref[...] + 0.5 * acc_ref[...]
        out = _layer_norm(y, g_ref[...], b_ref[...])
        o_ref[...] = out
        if emit_bf16:
            ob_ref[...] = out.astype(BF16)


def _ffn_ln(x, wg, wu, wd, g, b, *, emit_bf16, tm=512, tf=512):
    t, d = x.shape
    dff = wg.shape[1]
    tm = min(tm, t)
    n_f = dff // tf
    out_shape = [jax.ShapeDtypeStruct((t, d), F32)]
    out_specs = [pl.BlockSpec((tm, d), lambda i, f: (i, 0))]
    if emit_bf16:
        out_shape.append(jax.ShapeDtypeStruct((t, d), BF16))
        out_specs.append(pl.BlockSpec((tm, d), lambda i, f: (i, 0)))
    return pl.pallas_call(
        functools.partial(_ffn_ln_kernel, n_f=n_f, emit_bf16=emit_bf16),
        grid=(t // tm, n_f),
        in_specs=[
            pl.BlockSpec((tm, d), lambda i, f: (i, 0)),
            pl.BlockSpec((d, tf), lambda i, f: (0, f)),
            pl.BlockSpec((d, tf), lambda i, f: (0, f)),
            pl.BlockSpec((tf, d), lambda i, f: (f, 0)),
            pl.BlockSpec((1, d), lambda i, f: (0, 0)),
            pl.BlockSpec((1, d), lambda i, f: (0, 0)),
        ],
        out_specs=out_specs,
        out_shape=out_shape,
        scratch_shapes=[pltpu.VMEM((tm, d), BF16), pltpu.VMEM((tm, d), F32)],
        compiler_params=_params("parallel", "arbitrary"),
        name="ffn_ln",
    )(x, wg, wu, wd, g, b)


def _qk_proj_kernel(x_ref, w_ref, cos_ref, s1_ref, s2_ref, o_ref, *, n_q, tn):
    j = pl.program_id(1)
    y = jnp.dot(x_ref[...], w_ref[...], preferred_element_type=F32)
    sc = jnp.where(j < n_q, Q_SCALE, 1.0).astype(F32)
    c = cos_ref[...] * sc
    s1 = s1_ref[...] * sc
    s2 = s2_ref[...] * sc
    half = ROT_DIM // 2
    for k in range(tn // LANES):
        yk = y[:, k * LANES:(k + 1) * LANES]
        r = yk * c + pltpu.roll(yk, half, 1) * s1 + pltpu.roll(yk, LANES - half, 1) * s2
        o_ref[:, k * LANES:(k + 1) * LANES] = r.astype(o_ref.dtype)


def _qk_proj(hb, w_qk, cos_t, s1_t, s2_t, *, seq, tm=1024, tn=1024):
    t, d = hb.shape
    n = w_qk.shape[1]
    tm = min(tm, seq)
    tiles_per_seq = seq // tm
    tab = pl.BlockSpec((tm, LANES), lambda i, j: (i % tiles_per_seq, 0))
    return pl.pallas_call(
        functools.partial(_qk_proj_kernel, n_q=QK_K // tn, tn=tn),
        grid=(t // tm, n // tn),
        in_specs=[
            pl.BlockSpec((tm, d), lambda i, j: (i, 0)),
            pl.BlockSpec((d, tn), lambda i, j: (0, j)),
            tab, tab, tab,
        ],
        out_specs=pl.BlockSpec((tm, tn), lambda i, j: (i, j)),
        out_shape=jax.ShapeDtypeStruct((t, n), BF16),
        compiler_params=_params("parallel", "arbitrary"),
        name="qk_proj",
    )(hb, w_qk, cos_t, s1_t, s2_t)


def _proj_kernel(x_ref, w_ref, o_ref):
    o_ref[...] = jnp.dot(x_ref[...], w_ref[...], preferred_element_type=F32).astype(o_ref.dtype)


def _proj(hb, w, out_dtype, *, name, tm=1024, tn=1024):
    t, d = hb.shape
    n = w.shape[1]
    tm = min(tm, t)
    tn = min(tn, n)
    return pl.pallas_call(
        _proj_kernel,
        grid=(t // tm, n // tn),
        in_specs=[pl.BlockSpec((tm, d), lambda i, j: (i, 0)), pl.BlockSpec((d, tn), lambda i, j: (0, j))],
        out_specs=pl.BlockSpec((tm, tn), lambda i, j: (i, j)),
        out_shape=jax.ShapeDtypeStruct((t, n), out_dtype),
        compiler_params=_params("parallel", "arbitrary"),
        name=name,
    )(hb, w)


HALO = 64
CONV_ROWS = 128


def _conv_kernel(cur_ref, prev_ref, next_ref, sh_ref, w_ref, b_ref, o_ref, ext_ref, *, tm, tiles_per_seq):
    i = pl.program_id(0)
    first = (i % tiles_per_seq) == 0
    last = (i % tiles_per_seq) == tiles_per_seq - 1
    ext_ref[0:HALO, :] = jnp.where(first, 0.0, prev_ref[...]).astype(BF16)
    ext_ref[HALO:HALO + tm, :] = cur_ref[...]
    ext_ref[HALO + tm:2 * HALO + tm, :] = jnp.where(last, 0.0, next_ref[...]).astype(BF16)
    pad = CONV_WIDTH // 2
    for rb in range(tm // CONV_ROWS):
        r0 = rb * CONV_ROWS
        win = ext_ref[r0:r0 + CONV_ROWS + 2 * HALO, :]
        acc = b_ref[...] + ext_ref[r0 + HALO:r0 + HALO + CONV_ROWS, :].astype(F32) * w_ref[pad:pad + 1, :]
        for w in range(CONV_WIDTH):
            if w != pad:
                shifted = jnp.dot(sh_ref[w], win, preferred_element_type=F32)
                acc = acc + shifted * w_ref[w:w + 1, :]
        o_ref[r0:r0 + CONV_ROWS, :] = (acc * _sigmoid(acc)).astype(o_ref.dtype)


def _conv_silu(xbc, shifts, conv_w, conv_b, *, seq, tm=256):
    t = xbc.shape[0]
    tm = min(tm, seq)
    tiles_per_seq = seq // tm
    r = tm // HALO
    nblk = t // HALO
    return pl.pallas_call(
        functools.partial(_conv_kernel, tm=tm, tiles_per_seq=tiles_per_seq),
        grid=(t // tm,),
        in_specs=[
            pl.BlockSpec((tm, CONV_DIM), lambda i: (i, 0)),
            pl.BlockSpec((HALO, CONV_DIM), lambda i: (jnp.maximum(i * r - 1, 0), 0)),
            pl.BlockSpec((HALO, CONV_DIM), lambda i: (jnp.minimum((i + 1) * r, nblk - 1), 0)),
            pl.BlockSpec((CONV_WIDTH, CONV_ROWS, CONV_ROWS + 2 * HALO), lambda i: (0, 0, 0)),
            pl.BlockSpec((8, CONV_DIM), lambda i: (0, 0)),
            pl.BlockSpec((1, CONV_DIM), lambda i: (0, 0)),
        ],
        out_specs=pl.BlockSpec((tm, CONV_DIM), lambda i: (i, 0)),
        out_shape=jax.ShapeDtypeStruct((t, CONV_DIM), BF16),
        scratch_shapes=[pltpu.VMEM((tm + 2 * HALO, CONV_DIM), BF16)],
        compiler_params=_params("parallel"),
        name="conv_silu",
    )(xbc, xbc, xbc, shifts, conv_w, conv_b)


def _split_dot(v, m01, terms):
    out = None
    r = v
    for _ in range(terms):
        part = r.astype(BF16)
        d = jnp.dot(part, m01, preferred_element_type=F32)
        out = d if out is None else out + d
        r = r - part.astype(F32)
    return out


def _ssd_kernel(xbc_ref, dt_ref, par_ref, e_ref, y_ref, h_ref):
    L = SSD_CHUNK
    d = pl.program_id(1)
    c = pl.program_id(2)

    @pl.when(c == 0)
    def _init():
        h_ref[...] = jnp.zeros_like(h_ref)

    fwd = d == 0
    pre = dt_ref[...] + par_ref[0:1, :]
    dt = jnp.maximum(pre, 0.0) + jnp.log(1.0 + jnp.exp(-jnp.abs(pre)))
    a = dt * (-jnp.exp(par_ref[1:2, :]))

    row = lax.broadcasted_iota(jnp.int32, (L, L), 0)
    col = lax.broadcasted_iota(jnp.int32, (L, L), 1)
    msk = (col - row) * (1 - 2 * d) <= 0
    tri = jnp.where(msk, 1.0, 0.0).astype(BF16)
    c_col = _tri_cumsum(tri, a)
    c_row = c_col.T
    dt_row = dt.T
    tot = jnp.where(fwd, c_col[L - 1:L, :], c_col[0:1, :])
    e_out = jnp.exp(c_col)
    f_in = dt * jnp.exp(tot - c_col)
    cd = jnp.broadcast_to(jnp.exp(tot), (8, LANES))
    ex = _split_dot(jnp.concatenate([f_in, e_out], axis=0), e_ref[...], 1)
    f_x = ex[0:L]
    e_x = ex[L:2 * L]
    cd_x = _split_dot(cd, e_ref[...], 2)[0:1]

    n_x = SSM_HEADS * SSM_HEAD_DIM
    gn = SSM_GROUPS * SSM_STATE
    xw = (xbc_ref[:, 0:n_x].astype(F32) * f_x).astype(BF16)
    lane = lax.broadcasted_iota(jnp.int32, (L, LANES), 1)
    gw = n_x // SSM_GROUPS
    for g in range(SSM_GROUPS):
        bg = xbc_ref[:, n_x + g * SSM_STATE:n_x + (g + 1) * SSM_STATE]
        cg = xbc_ref[:, n_x + gn + g * SSM_STATE:n_x + gn + (g + 1) * SSM_STATE]
        bgt = bg.astype(F32).T.astype(BF16)
        gmat = jnp.dot(cg, bgt, preferred_element_type=F32)
        hg = h_ref[:, g * gw:(g + 1) * gw]
        yoff = jnp.dot(cg, hg.astype(BF16), preferred_element_type=F32) * e_x[:, g * gw:(g + 1) * gw]
        st = jnp.dot(bgt, xw[:, g * gw:(g + 1) * gw], preferred_element_type=F32)
        h_ref[:, g * gw:(g + 1) * gw] = cd_x[:, g * gw:(g + 1) * gw] * hg + st
        for jp in range(gw // LANES):
            lo = g * gw + jp * LANES
            xp = xbc_ref[:, lo:lo + LANES]
            outs = []
            for hh in range(2):
                h = lo // SSM_HEAD_DIM + hh
                seg = c_col[:, h:h + 1] - c_row[h:h + 1, :]
                decay = jnp.exp(jnp.where(msk, seg, -jnp.inf))
                m = (gmat * (decay * dt_row[h:h + 1, :])).astype(BF16)
                outs.append(jnp.dot(m, xp, preferred_element_type=F32))
            ydiag = jnp.where(lane < SSM_HEAD_DIM, outs[0], outs[1])
            y_ref[:, lo:lo + LANES] = (ydiag + yoff[:, jp * LANES:(jp + 1) * LANES]).astype(y_ref.dtype)


def _tri_cumsum(tri, a):
    out = None
    r = a
    for _ in range(3):
        part = r.astype(BF16)
        d = jnp.dot(tri, part, preferred_element_type=F32)
        out = d if out is None else out + d
        r = r - part.astype(F32)
    return out


def _ssd(xbc_act, dt_raw, par, expand, *, batch, seq):
    t = xbc_act.shape[0]
    L = SSD_CHUNK
    nc = seq // L
    n_x = SSM_HEADS * SSM_HEAD_DIM

    def chunk(b, d, c):
        return b * nc + c + d * (nc - 1 - 2 * c)

    return pl.pallas_call(
        _ssd_kernel,
        grid=(batch, 2, nc),
        in_specs=[
            pl.BlockSpec((L, CONV_DIM), lambda b, d, c: (chunk(b, d, c), 0)),
            pl.BlockSpec((L, LANES), lambda b, d, c: (chunk(b, d, c), d)),
            pl.BlockSpec((8, LANES), lambda b, d, c: (0, d)),
            pl.BlockSpec((LANES, n_x), lambda b, d, c: (0, 0)),
        ],
        out_specs=pl.BlockSpec((None, L, n_x), lambda b, d, c: (d, chunk(b, d, c), 0)),
        out_shape=jax.ShapeDtypeStruct((2, t, n_x), BF16),
        scratch_shapes=[pltpu.VMEM((SSM_STATE, n_x), F32)],
        compiler_params=_params("parallel", "parallel", "arbitrary"),
        name="ssd",
    )(xbc_act, dt_raw, par, expand)


STALE_MAX_LIMIT = 64.0


def _attn_block(q_ref, k_ref, v_ref, st_in, st_out, gap_ref, *, stale, tq, tk, rows, k0=0):
    m_in, l_in, acc_in = st_in
    m_out, l_out, acc_out = st_out
    if stale:
        m_bak, l_bak, acc_bak = st_out
        m_out, l_out, acc_out = st_in
    v = v_ref[pl.ds(k0, tk), :]
    n_lt = tk // LANES
    gap = None
    for rb in range(tq // rows):
        r0 = rb * rows
        for mi in range(2):
            q = q_ref[r0:r0 + rows, mi * ATT_HEAD_DIM:(mi + 1) * ATT_HEAD_DIM]
            k = k_ref[pl.ds(k0, tk), mi * ATT_HEAD_DIM:(mi + 1) * ATT_HEAD_DIM]
            s = lax.dot_general(q, k, (((1,), (1,)), ((), ())), preferred_element_type=F32)
            m_prev = m_in[mi, r0:r0 + rows, :]
            pm = s[:, 0:LANES]
            for c in range(1, n_lt):
                pm = jnp.maximum(pm, s[:, c * LANES:(c + 1) * LANES])
            m_new = jnp.maximum(m_prev, jnp.max(pm, axis=1, keepdims=True))
            alpha = jnp.exp2(m_prev - m_new)
            m_exp = m_prev if stale else m_new
            lsum = None
            ps = []
            for c in range(n_lt):
                pc = jnp.exp2(s[:, c * LANES:(c + 1) * LANES] - m_exp)
                lsum = pc if lsum is None else lsum + pc
                ps.append(pc.astype(BF16))
            pv = jnp.dot(jnp.concatenate(ps, axis=1), v, preferred_element_type=F32)
            alpha2 = jnp.concatenate([alpha] * (ATT_V_DIM // LANES), axis=1)
            if stale:
                l_prev = l_in[mi, r0:r0 + rows, :]
                acc_prev = acc_in[mi, r0:r0 + rows, :]
                m_bak[mi, r0:r0 + rows, :] = m_prev
                l_bak[mi, r0:r0 + rows, :] = l_prev
                acc_bak[mi, r0:r0 + rows, :] = acc_prev
                l_out[mi, r0:r0 + rows, :] = alpha * (l_prev + lsum)
                acc_out[mi, r0:r0 + rows, :] = alpha2 * (acc_prev + pv)
                g = pm - m_prev
                gap = g if gap is None else jnp.maximum(gap, g)
            else:
                l_out[mi, r0:r0 + rows, :] = alpha * l_in[mi, r0:r0 + rows, :] + lsum
                acc_out[mi, r0:r0 + rows, :] = alpha2 * acc_in[mi, r0:r0 + rows, :] + pv
            m_out[mi, r0:r0 + rows, :] = m_new
    if stale:
        gap_ref[...] = gap


def _attn_kernel(q_ref, k_ref, v_ref, lam_ref, nw_ref, o_ref, m_ref, l_ref, acc_ref, gap_ref,
                 *, n_kv, tq, tk, tk_exact, rows):
    kv = pl.program_id(3)
    live = (m_ref.at[0], l_ref.at[0], acc_ref.at[0])
    saved = (m_ref.at[1], l_ref.at[1], acc_ref.at[1])
    block = functools.partial(_attn_block, q_ref, k_ref, v_ref, gap_ref=gap_ref, tq=tq, rows=rows)

    @pl.when(kv == 0)
    def _init():
        for mi in range(2):
            dims = slice(mi * ATT_HEAD_DIM, (mi + 1) * ATT_HEAD_DIM)
            s0 = lax.dot_general(q_ref[:, dims], k_ref[0:LANES, dims], (((1,), (1,)), ((), ())),
                                 preferred_element_type=F32)
            m_ref[0, mi] = jnp.broadcast_to(jnp.max(s0, axis=1, keepdims=True), (tq, LANES))
        l_ref[0] = jnp.zeros(l_ref.shape[1:], F32)
        acc_ref[0] = jnp.zeros(acc_ref.shape[1:], F32)

    block(live, saved, stale=True, tk=tk)

    @pl.when(jnp.max(gap_ref[...]) > STALE_MAX_LIMIT)
    def _exact():
        for dst, src in zip(live, saved):
            dst[...] = src[...]

        def piece(i, carry):
            block(live, live, stale=False, tk=tk_exact, k0=pl.multiple_of(i * tk_exact, tk_exact))
            return carry

        lax.fori_loop(0, tk // tk_exact, piece, 0)

    @pl.when(kv == n_kv - 1)
    def _fin():
        lam = (jnp.exp(jnp.sum(lam_ref[0:1, :] * lam_ref[1:2, :], axis=1, keepdims=True))
               - jnp.exp(jnp.sum(lam_ref[2:3, :] * lam_ref[3:4, :], axis=1, keepdims=True)) + LAMBDA_INIT)
        l0 = jnp.sum(l_ref[0, 0], axis=1, keepdims=True)
        l1 = jnp.sum(l_ref[0, 1], axis=1, keepdims=True)
        o = acc_ref[0, 0] / l0 - lam * (acc_ref[0, 1] / l1)
        o = o * lax.rsqrt(jnp.mean(o * o, axis=1, keepdims=True) + NORM_EPS)
        o_ref[...] = (o * nw_ref[...] * (1.0 - LAMBDA_INIT)).astype(o_ref.dtype)


def _attention(qk, vzg, lam_par, norm_w, *, batch, seq, tq=2048, tk=2048, tk_exact=1024, rows=128):
    t = qk.shape[0]
    tq = min(tq, seq)
    tk = min(tk, seq)
    tk_exact = min(tk_exact, tk)
    rows = min(rows, tq)
    nq = seq // tq
    nk = seq // tk
    vd = ATT_V_DIM
    return pl.pallas_call(
        functools.partial(_attn_kernel, n_kv=nk, tq=tq, tk=tk, tk_exact=tk_exact, rows=rows),
        grid=(batch, ATT_HEADS, nq, nk),
        in_specs=[
            pl.BlockSpec((tq, vd), lambda b, h, i, j: (b * nq + i, QK_Q // vd + h)),
            pl.BlockSpec((tk, vd), lambda b, h, i, j: (b * nk + j, QK_K // vd + h)),
            pl.BlockSpec((tk, vd), lambda b, h, i, j: (b * nk + j, VZG_V // vd + h)),
            pl.BlockSpec((8, ATT_HEAD_DIM), lambda b, h, i, j: (0, 0)),
            pl.BlockSpec((1, vd), lambda b, h, i, j: (0, 0)),
        ],
        out_specs=pl.BlockSpec((tq, vd), lambda b, h, i, j: (b * nq + i, h)),
        out_shape=jax.ShapeDtypeStruct((t, ATT_HEADS * vd), BF16),
        scratch_shapes=[
            pltpu.VMEM((2, 2, tq, LANES), F32),
            pltpu.VMEM((2, 2, tq, LANES), F32),
            pltpu.VMEM((2, 2, tq, vd), F32),
            pltpu.VMEM((rows, LANES), F32),
        ],
        compiler_params=_params("parallel", "parallel", "parallel", "arbitrary"),
        name="diff_attention",
    )(qk, qk, vzg, lam_par, norm_w)


def _ssm_post_kernel(yf_ref, yb_ref, x_ref, z_ref, gs_ref, d_ref, nw_ref, w_ref, o_ref):
    y = yf_ref[...].astype(F32) + yb_ref[...].astype(F32) + d_ref[...] * x_ref[...].astype(F32)
    z = z_ref[...].astype(F32)
    y = y * (z * _sigmoid(z))
    gsz = y.shape[1] // SSM_GROUPS
    parts = []
    for g in range(SSM_GROUPS):
        seg = y[:, g * gsz:(g + 1) * gsz]
        parts.append(seg * lax.rsqrt(jnp.mean(seg * seg, axis=1, keepdims=True) + NORM_EPS))
    yn = jnp.concatenate(parts, axis=1) * nw_ref[...]
    ys = jnp.dot(yn.astype(BF16), w_ref[...], preferred_element_type=F32)
    o_ref[...] = _sigmoid(gs_ref[...].astype(F32)) * ys


def _ssm_post(y2, xbc_act, main, d_x, norm_w, w_br, *, tm=512):
    t = xbc_act.shape[0]
    tm = min(tm, t)
    d = D_MODEL
    row = lambda i: (i, 0)
    return pl.pallas_call(
        _ssm_post_kernel,
        grid=(t // tm,),
        in_specs=[
            pl.BlockSpec((None, tm, d), lambda i: (0, i, 0)),
            pl.BlockSpec((None, tm, d), lambda i: (1, i, 0)),
            pl.BlockSpec((tm, d), row),
            pl.BlockSpec((tm, d), lambda i: (i, VZG_Z // d)),
            pl.BlockSpec((tm, d), lambda i: (i, VZG_GS // d)),
            pl.BlockSpec((1, d), lambda i: (0, 0)),
            pl.BlockSpec((1, d), lambda i: (0, 0)),
            pl.BlockSpec((d, d), lambda i: (0, 0)),
        ],
        out_specs=pl.BlockSpec((tm, d), row),
        out_shape=jax.ShapeDtypeStruct((t, d), F32),
        compiler_params=_params("parallel"),
        name="ssm_post",
    )(y2, y2, xbc_act, main, main, d_x, norm_w, w_br)


def _mix_ln_kernel(p1_ref, oa_ref, ga_ref, h_ref, wa_ref, wo_ref, g_ref, b_ref, o_ref, ob_ref):
    ya = jnp.dot(oa_ref[...], wa_ref[...], preferred_element_type=F32)
    merged = p1_ref[...] + _sigmoid(ga_ref[...].astype(F32)) * ya
    mix = jnp.dot(merged.astype(BF16), wo_ref[...], preferred_element_type=F32)
    out = _layer_norm(ALPHA * h_ref[...] + mix, g_ref[...], b_ref[...])
    o_ref[...] = out
    ob_ref[...] = out.astype(BF16)


def _mix_ln(p1, o_att, main, h1, w_att_br, w_out, g, b, *, tm=256):
    t, d = h1.shape
    tm = min(tm, t)
    row = lambda i: (i, 0)
    const = lambda i: (0, 0)
    return pl.pallas_call(
        _mix_ln_kernel,
        grid=(t // tm,),
        in_specs=[
            pl.BlockSpec((tm, d), row),
            pl.BlockSpec((tm, d), row),
            pl.BlockSpec((tm, d), lambda i: (i, VZG_GA // d)),
            pl.BlockSpec((tm, d), row),
            pl.BlockSpec((d, d), const),
            pl.BlockSpec((d, d), const),
            pl.BlockSpec((1, d), const),
            pl.BlockSpec((1, d), const),
        ],
        out_specs=[pl.BlockSpec((tm, d), row), pl.BlockSpec((tm, d), row)],
        out_shape=[jax.ShapeDtypeStruct((t, d), F32), jax.ShapeDtypeStruct((t, d), BF16)],
        compiler_params=_params("parallel"),
        name="mix_ln",
    )(p1, o_att, main, h1, w_att_br, w_out, g, b)


def _rope_tables(seq):
    half = ROT_DIM // 2
    pos = jnp.arange(seq, dtype=F32)
    inv_freq = ROPE_THETA ** (-jnp.arange(0, ROT_DIM, 2, dtype=F32) / ROT_DIM)
    ang = pos[:, None] * inv_freq[None, :]
    cos, sin = jnp.cos(ang), jnp.sin(ang)
    zeros = jnp.zeros((seq, LANES - ROT_DIM), F32)
    zh = jnp.zeros((seq, half), F32)
    cos_t = jnp.concatenate([cos, cos, jnp.ones((seq, LANES - ROT_DIM), F32)], axis=1)
    s1_t = jnp.concatenate([zh, sin, zeros], axis=1)
    s2_t = jnp.concatenate([-sin, zh, zeros], axis=1)
    return cos_t, s1_t, s2_t


def _row(v, n=None):
    v = v.astype(F32).reshape(1, -1)
    if n is not None and v.shape[1] < n:
        v = jnp.pad(v, ((0, 0), (0, n - v.shape[1])))
    return v


def _layer(x2, batch, seq, ffn1_w_gate, ffn1_w_up, ffn1_w_down, ln1_g, ln1_b, w_in, conv_w, conv_b,
           dt_bias_f, dt_bias_b, a_log_f, a_log_b, d_skip, ssm_norm_w, w_ssm_br,
           lambda_q1, lambda_k1, lambda_q2, lambda_k2, attn_norm_w, w_att_br, w_out,
           ln2_g, ln2_b, ffn2_w_gate, ffn2_w_up, ffn2_w_down, ln3_g, ln3_b):
    d = D_MODEL
    o_z, o_xbc, o_dtf, o_dtb = 0, d, d + CONV_DIM, d + CONV_DIM + SSM_HEADS
    o_q = o_dtb + SSM_HEADS
    o_k, o_v, o_gs, o_ga = o_q + d, o_q + 2 * d, o_q + 3 * d, o_q + 4 * d
    cols = lambda lo, n: w_in[:, lo:lo + n]
    w_qk = cols(o_q, 2 * d).astype(BF16)
    w_vzg = jnp.concatenate([cols(o_v, d), cols(o_z, d), cols(o_gs, 2 * d)], axis=1).astype(BF16)
    w_xbc = cols(o_xbc, CONV_DIM).astype(BF16)
    pad = jnp.zeros((d, LANES - SSM_HEADS), w_in.dtype)
    w_dt = jnp.concatenate([cols(o_dtf, SSM_HEADS), pad, cols(o_dtb, SSM_HEADS), pad], axis=1).astype(BF16)

    h1, h1b = _ffn_ln(x2, ffn1_w_gate.astype(BF16), ffn1_w_up.astype(BF16), ffn1_w_down.astype(BF16),
                      _row(ln1_g), _row(ln1_b), emit_bf16=True)

    cos_t, s1_t, s2_t = _rope_tables(seq)
    qk = _qk_proj(h1b, w_qk, cos_t, s1_t, s2_t, seq=seq)
    vzg = _proj(h1b, w_vzg, BF16, name="vzg_proj")
    xbc = _proj(h1b, w_xbc, BF16, name="xbc_proj")
    dt_raw = _proj(h1b, w_dt, F32, name="dt_proj")

    cw = jnp.pad(conv_w.astype(F32).reshape(CONV_WIDTH, CONV_DIM), ((0, 8 - CONV_WIDTH), (0, 0)))
    win_col = jnp.arange(CONV_ROWS + 2 * HALO)[None, None, :]
    src_col = jnp.arange(CONV_ROWS)[None, :, None] + HALO - CONV_WIDTH // 2 + jnp.arange(CONV_WIDTH)[:, None, None]
    shifts = (win_col == src_col).astype(BF16)
    xbc_act = _conv_silu(xbc, shifts, cw, _row(conv_b), seq=seq)

    par = jnp.zeros((8, 2 * LANES), F32)
    par = par.at[0, 0:SSM_HEADS].set(dt_bias_f.astype(F32)).at[0, LANES:LANES + SSM_HEADS].set(dt_bias_b.astype(F32))
    par = par.at[1, 0:SSM_HEADS].set(a_log_f.astype(F32)).at[1, LANES:LANES + SSM_HEADS].set(a_log_b.astype(F32))
    n_x = SSM_HEADS * SSM_HEAD_DIM
    expand = (jnp.arange(LANES)[:, None] == (jnp.arange(n_x)[None, :] // SSM_HEAD_DIM)).astype(BF16)
    y2 = _ssd(xbc_act, dt_raw, par, expand, batch=batch, seq=seq)

    lam_par = jnp.zeros((8, ATT_HEAD_DIM), F32)
    for r, v in enumerate((lambda_q1, lambda_k1, lambda_q2, lambda_k2)):
        lam_par = lam_par.at[r].set(v.astype(F32))
    o_att = _attention(qk, vzg, lam_par, _row(attn_norm_w), batch=batch, seq=seq)

    d_x = jnp.repeat(d_skip.astype(F32), SSM_HEAD_DIM).reshape(1, n_x)
    p1 = _ssm_post(y2, xbc_act, vzg, d_x, _row(ssm_norm_w), w_ssm_br.astype(BF16))
    h2, h2b = _mix_ln(p1, o_att, vzg, h1, w_att_br.astype(BF16), w_out.astype(BF16), _row(ln2_g), _row(ln2_b))
    del h2b
    (out,) = _ffn_ln(h2, ffn2_w_gate.astype(BF16), ffn2_w_up.astype(BF16), ffn2_w_down.astype(BF16),
                     _row(ln3_g), _row(ln3_b), emit_bf16=False)
    return out


def kernel(x, ffn1_w_gate, ffn1_w_up, ffn1_w_down, ln1_g, ln1_b, w_in, conv_w, conv_b, dt_bias_f, dt_bias_b, a_log_f, a_log_b, d_skip, ssm_norm_w, w_ssm_br, lambda_q1, lambda_k1, lambda_q2, lambda_k2, attn_norm_w, w_att_br, w_out, ln2_g, ln2_b, ffn2_w_gate, ffn2_w_up, ffn2_w_down, ln3_g, ln3_b):
    batch, seq, d = x.shape
    assert d == D_MODEL and seq % SSD_CHUNK == 0
    params = (ffn1_w_gate, ffn1_w_up, ffn1_w_down, ln1_g, ln1_b, w_in, conv_w, conv_b, dt_bias_f, dt_bias_b,
              a_log_f, a_log_b, d_skip, ssm_norm_w, w_ssm_br, lambda_q1, lambda_k1, lambda_q2, lambda_k2,
              attn_norm_w, w_att_br, w_out, ln2_g, ln2_b, ffn2_w_gate, ffn2_w_up, ffn2_w_down, ln3_g, ln3_b)
    assert all(p.shape[0] == DEPTH for p in params)
    out = _layer(x.reshape(batch * seq, d), batch, seq, *[p[0] for p in params])
    return out.reshape(batch, seq, d).astype(x.dtype)
```
